```python
import math
import jax, jax.numpy as jnp
from jax import lax
import numpy as np

D_MODEL = 2048
BATCH = 2
SEQ = 8192
DEPTH = 1

D_MIX = D_MODEL
D_SSM = D_MIX // 2
D_ATTN = D_MIX - D_SSM
HEAD_DIM = 64
N_HEADS = D_ATTN // HEAD_DIM
SSM_GROUP = 16
N_SSM_GROUPS = D_SSM // SSM_GROUP
SSM_STATE = 64
DILATIONS = ((128, 1), (512, 4), (2048, 16))
D_FF = ((8 * D_MODEL // 3 + 127) // 128) * 128
DT_MIN = 0.001
DT_MAX = 0.1
EPS = 1e-6
NEG_INF = -1e30

kernel_name = "hymba_s5_longnet_macaron_block"


def rmsnorm(x, g):
    x32 = x.astype(jnp.float32)
    y = x32 * lax.rsqrt(jnp.mean(x32 * x32, axis=-1, keepdims=True) + EPS) * g.astype(jnp.float32)
    return y.astype(x.dtype)


def swiglu(x, w_gate, w_up, w_down):
    return (jax.nn.silu(x @ w_gate) * (x @ w_up)) @ w_down


def _ssm_combine(left, right):
    ar_i, ai_i, br_i, bi_i = left
    ar_j, ai_j, br_j, bi_j = right
    ar = ar_j * ar_i - ai_j * ai_i
    ai = ar_j * ai_i + ai_j * ar_i
    br = ar_j * br_i - ai_j * bi_i + br_j
    bi = ar_j * bi_i + ai_j * br_i + bi_j
    return (ar, ai, br, bi)


def s5_mixer(u, log_dt, a_re, a_im, b_re, b_im, c_re, c_im, d, w_glu, b_glu):
    Bsz, S, _ = u.shape
    f32 = jnp.float32
    u32 = u.astype(f32).reshape(Bsz, S, N_SSM_GROUPS, SSM_GROUP)
    lr = a_re.astype(f32)
    li = a_im.astype(f32)
    dt = jnp.exp(log_dt.astype(f32))[:, None]
    mag = jnp.exp(lr * dt)
    lb_re = mag * jnp.cos(li * dt)
    lb_im = mag * jnp.sin(li * dt)
    den = lr * lr + li * li
    nr = lb_re - 1.0
    ni = lb_im
    f_re = (nr * lr + ni * li) / den
    f_im = (ni * lr - nr * li) / den
    br = b_re.astype(f32)
    bi = b_im.astype(f32)
    bb_re = f_re[..., None] * br - f_im[..., None] * bi
    bb_im = f_re[..., None] * bi + f_im[..., None] * br
    bu_re = jnp.einsum('bsgh,gph->sbgp', u32, bb_re)
    bu_im = jnp.einsum('bsgh,gph->sbgp', u32, bb_im)
    a_re_t = jnp.broadcast_to(lb_re[None, None], (S, 1, N_SSM_GROUPS, SSM_STATE))
    a_im_t = jnp.broadcast_to(lb_im[None, None], (S, 1, N_SSM_GROUPS, SSM_STATE))
    _, _, s_re, s_im = lax.associative_scan(_ssm_combine, (a_re_t, a_im_t, bu_re, bu_im), axis=0)
    y = (jnp.einsum('sbgp,ghp->bsgh', s_re, c_re.astype(f32))
         - jnp.einsum('sbgp,ghp->bsgh', s_im, c_im.astype(f32))
         + d.astype(f32) * u32)
    y = jax.nn.gelu(y).reshape(Bsz, S, D_SSM)
    return y * jax.nn.sigmoid(y @ w_glu.astype(f32) + b_glu.astype(f32))


def dilated_branch(q, k, v, window, dilation):
    Bsz, S, H, E = q.shape
    nk = window // dilation
    blk = nk
    L = S // dilation
    nb = -(-L // blk)
    Lp = nb * blk

    def gather(t):
        t = t.reshape(Bsz, L, dilation, H, E).transpose(0, 2, 1, 3, 4)
        t = jnp.pad(t, ((0, 0), (0, 0), (0, Lp - L), (0, 0), (0, 0)))
        return t.reshape(Bsz, dilation, nb, blk, H, E)

    qb, kb, vb = gather(q), gather(k), gather(v)

    def with_prev(t):
        prev = jnp.pad(t, ((0, 0), (0, 0), (1, 0), (0, 0), (0, 0), (0, 0)))[:, :, :-1]
        return jnp.concatenate([prev, t], axis=3)

    kk, vv = with_prev(kb), with_prev(vb)
    s = jnp.einsum('bgnqhe,bgnkhe->bgnhqk', qb, kk) * (1.0 / math.sqrt(E))
    qi = jnp.arange(blk)[:, None]
    kj = jnp.arange(2 * blk)[None, :]
    dist = blk + qi - kj
    kpos = (jnp.arange(nb)[:, None, None] - 1) * blk + kj[None]
    valid = (dist >= 0)[None] & (dist <= nk)[None] & (kpos >= 0)
    s = jnp.where(valid[None, None, :, None], s, NEG_INF)
    m = jnp.max(s, axis=-1, keepdims=True)
    p = jnp.exp(s - m)
    den = jnp.sum(p, axis=-1, keepdims=True)
    o = jnp.einsum('bgnhqk,bgnkhe->bgnqhe', p / den, vv)
    lse = (m + jnp.log(den))[..., 0].transpose(0, 1, 2, 4, 3)
    o = o.reshape(Bsz, dilation, Lp, H, E)[:, :, :L].transpose(0, 2, 1, 3, 4).reshape(Bsz, S, H, E)
    lse = lse.reshape(Bsz, dilation, Lp, H)[:, :, :L].transpose(0, 2, 1, 3).reshape(Bsz, S, H)
    return o, lse


def dilated_attention(q, k, v):
    Bsz, S, _ = q.shape
    f32 = jnp.float32
    q = q.astype(f32).reshape(Bsz, S, N_HEADS, HEAD_DIM)
    k = k.astype(f32).reshape(Bsz, S, N_HEADS, HEAD_DIM)
    v = v.astype(f32).reshape(Bsz, S, N_HEADS, HEAD_DIM)
    outs, lses = [], []
    for window, dilation in DILATIONS:
        o, lse = dilated_branch(q, k, v, window, dilation)
        outs.append(o)
        lses.append(lse)
    w = jax.nn.softmax(jnp.stack(lses, axis=0), axis=0)
    o = jnp.sum(w[..., None] * jnp.stack(outs, axis=0), axis=0)
    return o.reshape(Bsz, S, D_ATTN)


def setup_inputs(seed: int = 0) -> dict:
    key = jax.random.key(seed)
    ks = jax.random.split(key, 32)
    f32 = jnp.float32
    G, P, Hg = N_SSM_GROUPS, SSM_STATE, SSM_GROUP

    def nrm(k, shape, scale):
        return jax.random.normal(k, shape, f32) * scale

    def gain(k, shape):
        return 1.0 + 0.01 * jax.random.normal(k, shape, f32)

    a_im = jnp.broadcast_to(math.pi * jnp.arange(P, dtype=f32), (DEPTH, G, P))
    a_im = a_im + 0.01 * jax.random.normal(ks[9], (DEPTH, G, P), f32)
    return {
        "x": jax.random.normal(ks[0], (BATCH, SEQ, D_MODEL), f32),
        "ffn1_norm": gain(ks[1], (DEPTH, D_MODEL)),
        "ffn1_w_gate": nrm(ks[2], (DEPTH, D_MODEL, D_FF), D_MODEL ** -0.5),
        "ffn1_w_up": nrm(ks[3], (DEPTH, D_MODEL, D_FF), D_MODEL ** -0.5),
        "ffn1_w_down": nrm(ks[4], (DEPTH, D_FF, D_MODEL), D_FF ** -0.5),
        "mix_norm": gain(ks[5], (DEPTH, D_MODEL)),
        "w_in": nrm(ks[6], (DEPTH, D_MODEL, 3 * D_ATTN + D_SSM), D_MODEL ** -0.5),
        "ssm_log_dt": jax.random.uniform(ks[7], (DEPTH, G), f32, math.log(DT_MIN), math.log(DT_MAX)),
        "ssm_a_re": -0.5 + 0.01 * jax.random.normal(ks[8], (DEPTH, G, P), f32),
        "ssm_a_im": a_im,
        "ssm_b_re": nrm(ks[10], (DEPTH, G, P, Hg), (2 * Hg) ** -0.5),
        "ssm_b_im": nrm(ks[11], (DEPTH, G, P, Hg), (2 * Hg) ** -0.5),
        "ssm_c_re": nrm(ks[12], (DEPTH, G, Hg, P), (2 * P) ** -0.5),
        "ssm_c_im": nrm(ks[13], (DEPTH, G, Hg, P), (2 * P) ** -0.5),
        "ssm_d": nrm(ks[14], (DEPTH, G, Hg), 1.0),
        "ssm_w_glu": nrm(ks[15], (DEPTH, D_SSM, D_SSM), D_SSM ** -0.5),
        "ssm_b_glu": nrm(ks[16], (DEPTH, D_SSM), 0.01),
        "ssm_out_norm": gain(ks[17], (DEPTH, D_SSM)),
        "attn_out_norm": gain(ks[18], (DEPTH, D_ATTN)),
        "w_out": nrm(ks[19], (DEPTH, D_MIX, D_MODEL), D_MIX ** -0.5),
        "ffn2_norm": gain(ks[20], (DEPTH, D_MODEL)),
        "ffn2_w_gate": nrm(ks[21], (DEPTH, D_MODEL, D_FF), D_MODEL ** -0.5),
        "ffn2_w_up": nrm(ks[22], (DEPTH, D_MODEL, D_FF), D_MODEL ** -0.5),
        "ffn2_w_down": nrm(ks[23], (DEPTH, D_FF, D_MODEL), D_FF ** -0.5),
        "final_norm": gain(ks[24], (D_MODEL,)),
    }


def reference(x, ffn1_norm, ffn1_w_gate, ffn1_w_up, ffn1_w_down, mix_norm, w_in,
              ssm_log_dt, ssm_a_re, ssm_a_im, ssm_b_re, ssm_b_im, ssm_c_re, ssm_c_im, ssm_d,
              ssm_w_glu, ssm_b_glu, ssm_out_norm, attn_out_norm, w_out,
              ffn2_norm, ffn2_w_gate, ffn2_w_up, ffn2_w_down, final_norm):
    h = x
    for l in range(DEPTH):
        h = h + 0.5 * swiglu(rmsnorm(h, ffn1_norm[l]), ffn1_w_gate[l], ffn1_w_up[l], ffn1_w_down[l])
        hn = rmsnorm(h, mix_norm[l])
        proj = hn @ w_in[l]
        q, k, v, u = jnp.split(proj, [D_ATTN, 2 * D_ATTN, 3 * D_ATTN], axis=-1)
        y_ssm = s5_mixer(u, ssm_log_dt[l], ssm_a_re[l], ssm_a_im[l], ssm_b_re[l], ssm_b_im[l],
                         ssm_c_re[l], ssm_c_im[l], ssm_d[l], ssm_w_glu[l], ssm_b_glu[l])
        y_att = dilated_attention(q, k, v)
        mixed = jnp.concatenate([rmsnorm(y_ssm, ssm_out_norm[l]).astype(h.dtype),
                                 rmsnorm(y_att, attn_out_norm[l]).astype(h.dtype)], axis=-1)
        h = h + mixed @ w_out[l]
        h = h + 0.5 * swiglu(rmsnorm(h, ffn2_norm[l]), ffn2_w_gate[l], ffn2_w_up[l], ffn2_w_down[l])
    return rmsnorm(h, final_norm)
```

```python
import functools

import jax
import jax.numpy as jnp
from jax import lax
from jax.experimental import pallas as pl
from jax.experimental.pallas import tpu as pltpu

F32 = jnp.float32
BF16 = jnp.bfloat16

EPS = 1e-6
NEG_INF = -1e30
HEAD_DIM = 64
SSM_GROUP = 16
DILATIONS = ((128, 1), (512, 4), (2048, 16))

V7X_LANES = 128
V7X_VMEM_BYTES = 64 * 1024 * 1024
VMEM_LIMIT_BYTES = V7X_VMEM_BYTES - 6 * 1024 * 1024

SSM_CHUNK = 32
ATTN_SPAN = 2048
ATTN_BLOCK = 128


def _rmsnorm(x, g):
    return x * lax.rsqrt(jnp.mean(x * x, axis=-1, keepdims=True) + EPS) * g


def _params(*semantics):
    return pltpu.CompilerParams(dimension_semantics=semantics, vmem_limit_bytes=VMEM_LIMIT_BYTES)


def _ffn_kernel(*refs, n_f, n_chunks, final_norm):
    if final_norm:
        h_ref, g_ref, wg_ref, wu_ref, wd_ref, fg_ref, o_ref, n_ref = refs
    else:
        h_ref, g_ref, wg_ref, wu_ref, wd_ref, o_ref, n_ref = refs
    j = pl.program_id(1)

    @pl.when(j == 0)
    def _():
        h = h_ref[...]
        n_ref[...] = _rmsnorm(h, g_ref[...]).astype(BF16)
        o_ref[...] = h

    n = n_ref[...]
    gate = jnp.dot(n, wg_ref[...], preferred_element_type=F32)
    up = jnp.dot(n, wu_ref[...], preferred_element_type=F32)
    a = (0.5 * (gate * jax.nn.sigmoid(gate)) * up).astype(BF16)
    cw = o_ref.shape[1] // n_chunks
    for c in range(n_chunks):
        sl = slice(c * cw, (c + 1) * cw)
        o_ref[:, sl] += jnp.dot(a, wd_ref[:, sl], preferred_element_type=F32)

    if final_norm:
        @pl.when(j == n_f - 1)
        def _():
            o_ref[...] = _rmsnorm(o_ref[...], fg_ref[...])


def _ffn(h, norm_g, wg, wu, wd, final_g=None, *, tm=1024, tf=512):
    t, d = h.shape
    fp = wg.shape[1]
    n_f = fp // tf
    final_norm = final_g is not None
    row = lambda i, j: (i, 0)
    in_specs = [
        pl.BlockSpec((tm, d), row),
        pl.BlockSpec((1, d), lambda i, j: (0, 0)),
        pl.BlockSpec((d, tf), lambda i, j: (0, j)),
        pl.BlockSpec((d, tf), lambda i, j: (0, j)),
        pl.BlockSpec((tf, d), lambda i, j: (j, 0)),
    ]
    args = [h, norm_g.reshape(1, d), wg, wu, wd]
    if final_norm:
        in_specs.append(pl.BlockSpec((1, d), lambda i, j: (0, 0)))
        args.append(final_g.reshape(1, d))
    return pl.pallas_call(
        functools.partial(_ffn_kernel, n_f=n_f, n_chunks=4, final_norm=final_norm),
        grid=(t // tm, n_f),
        in_specs=in_specs,
        out_specs=pl.BlockSpec((tm, d), row),
        out_shape=jax.ShapeDtypeStruct((t, d), F32),
        scratch_shapes=[pltpu.VMEM((tm, d), BF16)],
        compiler_params=_params("parallel", "arbitrary"),
        name="ffn_final" if final_norm else "ffn",
    )(*args)


def _inproj_kernel(h_ref, g_ref, w_ref, q_ref, k_ref, v_ref, u_ref, n_ref):
    j = pl.program_id(1)

    @pl.when(j == 0)
    def _():
        n_ref[...] = _rmsnorm(h_ref[...], g_ref[...]).astype(BF16)

    res = jnp.dot(n_ref[...], w_ref[...], preferred_element_type=F32)
    for idx, ref in enumerate((q_ref, k_ref, v_ref, u_ref)):
        @pl.when(j == idx)
        def _(ref=ref):
            ref[...] = res.astype(ref.dtype)


def _inproj(h, norm_g, w_in, *, tm=1024):
    t, d = h.shape
    c = w_in.shape[1] // 4
    row = lambda i, j: (i, 0)
    return pl.pallas_call(
        _inproj_kernel,
        grid=(t // tm, 4),
        in_specs=[
            pl.BlockSpec((tm, d), row),
            pl.BlockSpec((1, d), lambda i, j: (0, 0)),
            pl.BlockSpec((d, c), lambda i, j: (0, j)),
        ],
        out_specs=[pl.BlockSpec((tm, c), row)] * 4,
        out_shape=[jax.ShapeDtypeStruct((t, c), BF16)] * 3 + [jax.ShapeDtypeStruct((t, c), F32)],
        scratch_shapes=[pltpu.VMEM((tm, d), BF16)],
        compiler_params=_params("parallel", "arbitrary"),
        name="inproj",
    )(h, norm_g.reshape(1, d), w_in)


def _ssm_discretise(log_dt, a_re, a_im, b_re, b_im):
    dt = jnp.exp(log_dt)[:, None]
    mag = jnp.exp(a_re * dt)
    lb_re = mag * jnp.cos(a_im * dt)
    lb_im = mag * jnp.sin(a_im * dt)
    den = a_re * a_re + a_im * a_im
    nr = lb_re - 1.0
    ni = lb_im
    f_re = (nr * a_re + ni * a_im) / den
    f_im = (ni * a_re - nr * a_im) / den
    bb_re = f_re[..., None] * b_re - f_im[..., None] * b_im
    bb_im = f_re[..., None] * b_im + f_im[..., None] * b_re
    return lb_re, lb_im, bb_re, bb_im


def _ssm_ops_kernel(par, pai, prr, pri, ctr, cti, btr, bti, bbtr, bbti, lr, li, kt_ref, w_ref, vt_ref):
    pa_r, pa_i = par[0], pai[0]
    pr_r, pr_i = prr[0], pri[0]
    ct_r, ct_i = ctr[0], cti[0]
    bt_r, bt_i = btr[0], bti[0]
    lam_r, lam_i = lr[0], li[0]
    hi = lax.Precision.HIGHEST
    q_r = pa_r * ct_r - pa_i * ct_i
    q_i = pa_r * ct_i + pa_i * ct_r
    kt_ref[0] = (jnp.dot(bbtr[0], q_r, precision=hi, preferred_element_type=F32)
                 - jnp.dot(bbti[0], q_i, precision=hi, preferred_element_type=F32))
    pb_r = pa_r * lam_r - pa_i * lam_i
    pb_i = pa_r * lam_i + pa_i * lam_r
    v_r = pb_r * ct_r - pb_i * ct_i
    v_i = -(pb_r * ct_i + pb_i * ct_r)
    vt_ref[0] = jnp.concatenate([v_r, v_i], axis=0).astype(BF16)
    w_r = pr_r * bt_r - pr_i * bt_i
    w_i = pr_r * bt_i + pr_i * bt_r
    w_ref[0] = jnp.concatenate([w_r, w_i], axis=0).astype(BF16)


def _toeplitz_kernel(kt_ref, mt_ref, *, group):
    tau = pl.program_id(0)
    g, _, n = kt_ref.shape
    row = lax.broadcasted_iota(jnp.int32, (n, n), 0)
    col = lax.broadcasted_iota(jnp.int32, (n, n), 1)
    shift = (col == row + group * tau).astype(BF16)
    kt = kt_ref[...].reshape(g * group, n).astype(BF16)
    res = jnp.dot(kt, shift, preferred_element_type=F32)
    mt_ref[...] = res.reshape(g, group, n).astype(BF16)


def _ssm_main_kernel(x_ref, mt_ref, w_ref, vt_ref, ar_ref, ai_ref, y_ref, *, n_chunks, n_steps):
    x = x_ref[0]
    y = jnp.dot(x, mt_ref[0], preferred_element_type=F32)
    s = lax.dot_general(x, w_ref[0], (((1,), (1,)), ((), ())), preferred_element_type=F32)
    half = s.shape[1] // 2
    chunk = lax.broadcasted_iota(jnp.int32, s.shape, 0) % n_chunks
    for j in range(n_steps):
        k = 1 << j
        sh = jnp.where(chunk >= k, pltpu.roll(s, k, 0), 0.0)
        s = s + ar_ref[0, j:j + 1, :] * sh + ai_ref[0, j:j + 1, :] * pltpu.roll(sh, half, 1)
    s_in = jnp.where(chunk >= 1, pltpu.roll(s, 1, 0), 0.0)
    y_ref[0] = y + jnp.dot(s_in.astype(BF16), vt_ref[0], preferred_element_type=F32)


def _ssm_conv(u, batch, log_dt, a_re, a_im, b_re, b_im, c_re, c_im):
    t, d_ssm = u.shape
    g, p = a_re.shape
    h = b_re.shape[-1]
    lc = SSM_CHUNK
    n = lc * h
    seq = t // batch
    n_chunks = seq // lc
    rows = batch * n_chunks
    n_steps = max(1, (n_chunks - 1).bit_length())

    lb_re, lb_im, bb_re, bb_im = _ssm_discretise(log_dt, a_re, a_im, b_re, b_im)
    pw_r, pw_i = [jnp.ones_like(lb_re)], [jnp.zeros_like(lb_re)]
    for _ in range(lc):
        pw_r, pw_i = (pw_r + [pw_r[-1] * lb_re - pw_i[-1] * lb_im],
                      pw_i + [pw_r[-1] * lb_im + pw_i[-1] * lb_re])
    pw_r = jnp.stack(pw_r, axis=-1)
    pw_i = jnp.stack(pw_i, axis=-1)
    lane_rep = lambda a: jnp.repeat(a, h, axis=-1)
    lane_tile = lambda a: jnp.tile(a, (1, 1, lc))
    tables = [
        lane_rep(pw_r[..., :lc]), lane_rep(pw_i[..., :lc]),
        lane_rep(pw_r[..., lc - 1::-1]), lane_rep(pw_i[..., lc - 1::-1]),
        lane_tile(c_re.transpose(0, 2, 1)), lane_tile(c_im.transpose(0, 2, 1)),
        lane_tile(bb_re), lane_tile(bb_im),
    ]
    sr, si = pw_r[..., lc], pw_i[..., lc]
    ar, ai = [], []
    for _ in range(n_steps):
        ar.append(jnp.concatenate([sr, sr], axis=-1))
        ai.append(jnp.concatenate([-si, si], axis=-1))
        sr, si = sr * sr - si * si, 2.0 * sr * si
    ar = jnp.stack(ar, axis=1)
    ai = jnp.stack(ai, axis=1)

    tab_spec = pl.BlockSpec((1, p, n), lambda i: (i, 0, 0))
    kt, w, vt = pl.pallas_call(
        _ssm_ops_kernel,
        grid=(g,),
        in_specs=[tab_spec] * 8 + [pl.BlockSpec((1, h, p), lambda i: (i, 0, 0))] * 2
        + [pl.BlockSpec((1, p, 1), lambda i: (i, 0, 0))] * 2,
        out_specs=[pl.BlockSpec((1, h, n), lambda i: (i, 0, 0)),
                   pl.BlockSpec((1, 2 * p, n), lambda i: (i, 0, 0)),
                   pl.BlockSpec((1, 2 * p, n), lambda i: (i, 0, 0))],
        out_shape=[jax.ShapeDtypeStruct((g, h, n), F32),
                   jax.ShapeDtypeStruct((g, 2 * p, n), BF16),
                   jax.ShapeDtypeStruct((g, 2 * p, n), BF16)],
        compiler_params=_params("parallel"),
        name="ssm_ops",
    )(*tables, bb_re.transpose(0, 2, 1), bb_im.transpose(0, 2, 1), lb_re[..., None], lb_im[..., None])

    mt = pl.pallas_call(
        functools.partial(_toeplitz_kernel, group=h),
        grid=(lc,),
        in_specs=[pl.BlockSpec((g, h, n), lambda i: (0, 0, 0))],
        out_specs=pl.BlockSpec((g, None, h, n), lambda i: (0, i, 0, 0)),
        out_shape=jax.ShapeDtypeStruct((g, lc, h, n), BF16),
        compiler_params=_params("parallel"),
        name="ssm_toeplitz",
    )(kt).reshape(g, n, n)

    x = u.astype(BF16).reshape(rows, lc, g, h).transpose(2, 0, 1, 3).reshape(g, rows, n)
    y = pl.pallas_call(
        functools.partial(_ssm_main_kernel, n_chunks=n_chunks, n_steps=n_steps),
        grid=(g,),
        in_specs=[pl.BlockSpec((1, rows, n), lambda i: (i, 0, 0)),
                  pl.BlockSpec((1, n, n), lambda i: (i, 0, 0)),
                  pl.BlockSpec((1, 2 * p, n), lambda i: (i, 0, 0)),
                  pl.BlockSpec((1, 2 * p, n), lambda i: (i, 0, 0)),
                  pl.BlockSpec((1, n_steps, 2 * p), lambda i: (i, 0, 0)),
                  pl.BlockSpec((1, n_steps, 2 * p), lambda i: (i, 0, 0))],
        out_specs=pl.BlockSpec((1, rows, n), lambda i: (i, 0, 0)),
        out_shape=jax.ShapeDtypeStruct((g, rows, n), F32),
        compiler_params=_params("parallel"),
        name="ssm_main",
    )(x, mt, w, vt, ar, ai)
    return y.reshape(g, rows, lc, h).transpose(1, 2, 0, 3).reshape(t, d_ssm)


def _attn_kernel(q_ref, k_ref, v_ref, o_ref, kf, vf, qf, acc, mm, ll, *, span, blk):
    s = pl.program_id(2)

    @pl.when(s == 0)
    def _():
        kf[...] = k_ref[0].astype(F32)
        vf[...] = v_ref[0].astype(F32)

    qf[...] = q_ref[0].astype(F32) * (1.0 / (HEAD_DIM ** 0.5))
    acc[...] = jnp.zeros_like(acc)
    ll[...] = jnp.zeros_like(ll)
    mm[...] = jnp.full_like(mm, NEG_INF)

    lanes = q_ref.shape[2]
    head0 = lax.broadcasted_iota(jnp.int32, (blk, lanes), 1) < HEAD_DIM
    rel = (lax.broadcasted_iota(jnp.int32, (blk, 2 * blk), 0)
           - lax.broadcasted_iota(jnp.int32, (blk, 2 * blk), 1))

    for _, dil in DILATIONS:
        per_res = span // dil
        n_blk = per_res // blk

        def body(it, carry, dil=dil, per_res=per_res, n_blk=n_blk):
            res = it // n_blk
            jj = it % n_blk
            qb = s * per_res + jj * blk
            kb = jnp.maximum(qb - blk, 0)
            q_rows = pl.ds(jj * (blk * dil) + res, blk, stride=dil)
            k_rows = pl.ds(res + dil * kb, 2 * blk, stride=dil)
            q = qf[q_rows, :]
            kk = kf[k_rows, :].astype(BF16)
            vv = vf[k_rows, :].astype(BF16)
            dist = rel + (qb - kb)
            valid = (dist >= 0) & (dist <= blk)
            outs, maxs, sums = [], [], []
            for head_mask in (head0, ~head0):
                qh = jnp.where(head_mask, q, 0.0).astype(BF16)
                sc = lax.dot_general(qh, kk, (((1,), (1,)), ((), ())), preferred_element_type=F32)
                sc = jnp.where(valid, sc, NEG_INF)
                m = jnp.max(sc, axis=-1, keepdims=True)
                p = jnp.exp(sc - m)
                sums.append(jnp.sum(p, axis=-1, keepdims=True))
                maxs.append(m)
                outs.append(jnp.dot(p.astype(BF16), vv, preferred_element_type=F32))
            o_blk = jnp.where(head0, outs[0], outs[1])
            m_blk = jnp.where(head0, maxs[0], maxs[1])
            l_blk = jnp.where(head0, sums[0], sums[1])
            m_old = mm[q_rows, :]
            m_new = jnp.maximum(m_old, m_blk)
            a_old = jnp.exp(m_old - m_new)
            a_blk = jnp.exp(m_blk - m_new)
            acc[q_rows, :] = acc[q_rows, :] * a_old + o_blk * a_blk
            ll[q_rows, :] = ll[q_rows, :] * a_old + l_blk * a_blk
            mm[q_rows, :] = m_new
            return carry

        lax.fori_loop(0, dil * n_blk, body, 0)

    o_ref[0] = (acc[...] / ll[...]).astype(o_ref.dtype)


def _attention(q, k, v, batch):
    t, c = q.shape
    seq = t // batch
    span, blk = ATTN_SPAN, ATTN_BLOCK
    assert all(w // d == blk and span % (blk * d) == 0 for w, d in DILATIONS)
    assert seq % span == 0 and V7X_LANES % HEAD_DIM == 0
    q3, k3, v3 = (a.reshape(batch, seq, c) for a in (q, k, v))
    lanes = V7X_LANES
    full = pl.BlockSpec((1, seq, lanes), lambda b, hp, s: (b, 0, hp))
    tile = pl.BlockSpec((1, span, lanes), lambda b, hp, s: (b, s, hp))
    out = pl.pallas_call(
        functools.partial(_attn_kernel, span=span, blk=blk),
        grid=(batch, c // lanes, seq // span),
        in_specs=[tile, full, full],
        out_specs=tile,
        out_shape=jax.ShapeDtypeStruct((batch, seq, c), BF16),
        scratch_shapes=[pltpu.VMEM((seq, lanes), F32), pltpu.VMEM((seq, lanes), F32),
                        pltpu.VMEM((span, lanes), F32), pltpu.VMEM((span, lanes), F32),
                        pltpu.VMEM((span, lanes), F32), pltpu.VMEM((span, lanes), F32)],
        compiler_params=_params("parallel", "parallel", "arbitrary"),
        name="attn",
    )(q3, k3, v3)
    return out.reshape(t, c)


def _post_kernel(yc_ref, u_ref, att_ref, h_ref, d_ref, wglu_ref, bglu_ref, gs_ref, ga_ref, wo_ref, o_ref):
    y = jax.nn.gelu(yc_ref[...] + d_ref[...] * u_ref[...])
    gate = jax.nn.sigmoid(jnp.dot(y.astype(BF16), wglu_ref[...], preferred_element_type=F32) + bglu_ref[...])
    ys = _rmsnorm(y * gate, gs_ref[...]).astype(BF16)
    ya = _rmsnorm(att_ref[...].astype(F32), ga_ref[...]).astype(BF16)
    d_ssm = ys.shape[1]
    o_ref[...] = (h_ref[...]
                  + jnp.dot(ys, wo_ref[:d_ssm, :], preferred_element_type=F32)
                  + jnp.dot(ya, wo_ref[d_ssm:, :], preferred_element_type=F32))


def _post(yc, u, att, h, d, w_glu, b_glu, g_ssm, g_att, w_out, *, tm=512):
    t, dm = h.shape
    c = yc.shape[1]
    row = lambda i: (i, 0)
    const = lambda i: (0, 0)
    return pl.pallas_call(
        _post_kernel,
        grid=(t // tm,),
        in_specs=[pl.BlockSpec((tm, c), row), pl.BlockSpec((tm, c), row), pl.BlockSpec((tm, c), row),
                  pl.BlockSpec((tm, dm), row),
                  pl.BlockSpec((1, c), const), pl.BlockSpec((c, c), const), pl.BlockSpec((1, c), const),
                  pl.BlockSpec((1, c), const), pl.BlockSpec((1, c), const), pl.BlockSpec((2 * c, dm), const)],
        out_specs=pl.BlockSpec((tm, dm), row),
        out_shape=jax.ShapeDtypeStruct((t, dm), F32),
        compiler_params=_params("parallel"),
        name="post",
    )(yc, u, att, h, d.reshape(1, c), w_glu, b_glu.reshape(1, c), g_ssm.reshape(1, c), g_att.reshape(1, c), w_out)


def _pad_axis(w, axis, mult):
    pad = [(0, 0), (0, 0)]
    pad[axis] = (0, (-w.shape[axis]) % mult)
    return jnp.pad(w, pad)


def kernel(x, ffn1_norm, ffn1_w_gate, ffn1_w_up, ffn1_w_down, mix_norm, w_in, ssm_log_dt, ssm_a_re, ssm_a_im, ssm_b_re, ssm_b_im, ssm_c_re, ssm_c_im, ssm_d, ssm_w_glu, ssm_b_glu, ssm_out_norm, attn_out_norm, w_out, ffn2_norm, ffn2_w_gate, ffn2_w_up, ffn2_w_down, final_norm):
    batch, seq, d_model = x.shape
    depth = ffn1_norm.shape[0]
    tf = 512
    h = x.reshape(batch * seq, d_model)
    for l in range(depth):
        last = l == depth - 1
        ffn_w = lambda wg, wu, wd: (_pad_axis(wg.astype(BF16), 1, tf), _pad_axis(wu.astype(BF16), 1, tf),
                                    _pad_axis(wd.astype(BF16), 0, tf))
        h = _ffn(h, ffn1_norm[l], *ffn_w(ffn1_w_gate[l], ffn1_w_up[l], ffn1_w_down[l]), tf=tf)
        q, k, v, u = _inproj(h, mix_norm[l], w_in[l].astype(BF16))
        yc = _ssm_conv(u, batch, ssm_log_dt[l], ssm_a_re[l], ssm_a_im[l], ssm_b_re[l], ssm_b_im[l],
                       ssm_c_re[l], ssm_c_im[l])
        att = _attention(q, k, v, batch)
        h = _post(yc, u, att, h, ssm_d[l], ssm_w_glu[l].astype(BF16), ssm_b_glu[l], ssm_out_norm[l],
                  attn_out_norm[l], w_out[l].astype(BF16))
        h = _ffn(h, ffn2_norm[l], *ffn_w(ffn2_w_gate[l], ffn2_w_up[l], ffn2_w_down[l]),
                 final_g=final_norm if last else None, tf=tf)
    return h.reshape(batch, seq, d_model)
```

```python
import functools

import jax
import jax.numpy as jnp
from jax import lax
from jax.experimental import pallas as pl
from jax.experimental.pallas import tpu as pltpu

F32 = jnp.float32
BF16 = jnp.bfloat16

EPS = 1e-6
NEG_INF = -1e30
HEAD_DIM = 64
SSM_GROUP = 16
DILATIONS = ((128, 1), (512, 4), (2048, 16))

V7X_LANES = 128
V7X_VMEM_BYTES = 64 * 1024 * 1024
VMEM_LIMIT_BYTES = V7X_VMEM_BYTES - 6 * 1024 * 1024

SSM_CHUNK = 32
ATTN_SPAN = 2048
ATTN_BLOCK = 128


def _rmsnorm(x, g):
    return x * lax.rsqrt(jnp.mean(x * x, axis=-1, keepdims=True) + EPS) * g


def _params(*semantics):
    return pltpu.CompilerParams(dimension_semantics=semantics, vmem_limit_bytes=VMEM_LIMIT_BYTES)


def _ffn_kernel(*refs, n_f, n_chunks, final_norm):
    if final_norm:
        h_ref, g_ref, wg_ref, wu_ref, wd_ref, fg_ref, o_ref, n_ref = refs
    else:
        h_ref, g_ref, wg_ref, wu_ref, wd_ref, o_ref, n_ref = refs
    j = pl.program_id(1)

    @pl.when(j == 0)
    def _():
        h = h_ref[...]
        n_ref[...] = _rmsnorm(h, g_ref[...]).astype(BF16)
        o_ref[...] = h

    n = n_ref[...]
    gate = jnp.dot(n, wg_ref[...], preferred_element_type=F32)
    up = jnp.dot(n, wu_ref[...], preferred_element_type=F32)
    a = (0.5 * (gate * jax.nn.sigmoid(gate)) * up).astype(BF16)
    cw = o_ref.shape[1] // n_chunks
    for c in range(n_chunks):
        sl = slice(c * cw, (c + 1) * cw)
        o_ref[:, sl] += jnp.dot(a, wd_ref[:, sl], preferred_element_type=F32)

    if final_norm:
        @pl.when(j == n_f - 1)
        def _():
            o_ref[...] = _rmsnorm(o_ref[...], fg_ref[...])


def _ffn(h, norm_g, wg, wu, wd, final_g=None, *, tm=1024, tf=512):
    t, d = h.shape
    fp = wg.shape[1]
    n_f = fp // tf
    final_norm = final_g is not None
    row = lambda i, j: (i, 0)
    in_specs = [
        pl.BlockSpec((tm, d), row),
        pl.BlockSpec((1, d), lambda i, j: (0, 0)),
        pl.BlockSpec((d, tf), lambda i, j: (0, j)),
        pl.BlockSpec((d, tf), lambda i, j: (0, j)),
        pl.BlockSpec((tf, d), lambda i, j: (j, 0)),
    ]
    args = [h, norm_g.reshape(1, d), wg, wu, wd]
    if final_norm:
        in_specs.append(pl.BlockSpec((1, d), lambda i, j: (0, 0)))
        args.append(final_g.reshape(1, d))
    return pl.pallas_call(
        functools.partial(_ffn_kernel, n_f=n_f, n_chunks=4, final_norm=final_norm),
        grid=(t // tm, n_f),
        in_specs=in_specs,
        out_specs=pl.BlockSpec((tm, d), row),
        out_shape=jax.ShapeDtypeStruct((t, d), F32),
        scratch_shapes=[pltpu.VMEM((tm, d), BF16)],
        compiler_params=_params("parallel", "arbitrary"),
        name="ffn_final" if final_norm else "ffn",
    )(*args)


def _inproj_kernel(h_ref, g_ref, w_ref, q_ref, k_ref, v_ref, u_ref, n_ref):
    j = pl.program_id(1)

    @pl.when(j == 0)
    def _():
        n_ref[...] = _rmsnorm(h_ref[...], g_ref[...]).astype(BF16)

    res = jnp.dot(n_ref[...], w_ref[...], preferred_element_type=F32)
    for idx, ref in enumerate((q_ref, k_ref, v_ref, u_ref)):
        @pl.when(j == idx)
        def _(ref=ref):
            ref[...] = res.astype(ref.dtype)


def _inproj(h, norm_g, w_in, *, tm=1024):
    t, d = h.shape
    c = w_in.shape[1] // 4
    row = lambda i, j: (i, 0)
    return pl.pallas_call(
        _inproj_kernel,
        grid=(t // tm, 4),
        in_specs=[
            pl.BlockSpec((tm, d), row),
            pl.BlockSpec((1, d), lambda i, j: (0, 0)),
            pl.BlockSpec((d, c), lambda i, j: (0, j)),
        ],
        out_specs=[pl.BlockSpec((tm, c), row)] * 4,
        out_shape=[jax.ShapeDtypeStruct((t, c), BF16)] * 3 + [jax.ShapeDtypeStruct((t, c), F32)],
        scratch_shapes=[pltpu.VMEM((tm, d), BF16)],
        compiler_params=_params("parallel", "arbitrary"),
        name="inproj",
    )(h, norm_g.reshape(1, d), w_in)


def _ssm_discretise(log_dt, a_re, a_im, b_re, b_im):
    dt = jnp.exp(log_dt)[:, None]
    mag = jnp.exp(a_re * dt)
    lb_re = mag * jnp.cos(a_im * dt)
    lb_im = mag * jnp.sin(a_im * dt)
    den = a_re * a_re + a_im * a_im
    nr = lb_re - 1.0
    ni = lb_im
    f_re = (nr * a_re + ni * a_im) / den
    f_im = (ni * a_re - nr * a_im) / den
    bb_re = f_re[..., None] * b_re - f_im[..., None] * b_im
    bb_im = f_re[..., None] * b_im + f_im[..., None] * b_re
    return lb_re, lb_im, bb_re, bb_im


def _ssm_ops_kernel(par, pai, prr, pri, ctr, cti, btr, bti, bbtr, bbti, lr, li, kt_ref, w_ref, vt_ref):
    pa_r, pa_i = par[0], pai[0]
    pr_r, pr_i = prr[0], pri[0]
    ct_r, ct_i = ctr[0], cti[0]
    bt_r, bt_i = btr[0], bti[0]
    lam_r, lam_i = lr[0], li[0]
    hi = lax.Precision.HIGHEST
    q_r = pa_r * ct_r - pa_i * ct_i
    q_i = pa_r * ct_i + pa_i * ct_r
    kt_ref[0] = (jnp.dot(bbtr[0], q_r, precision=hi, preferred_element_type=F32)
                 - jnp.dot(bbti[0], q_i, precision=hi, preferred_element_type=F32))
    pb_r = pa_r * lam_r - pa_i * lam_i
    pb_i = pa_r * lam_i + pa_i * lam_r
    v_r = pb_r * ct_r - pb_i * ct_i
    v_i = -(pb_r * ct_i + pb_i * ct_r)
    vt_ref[0] = jnp.concatenate([v_r, v_i], axis=0).astype(BF16)
    w_r = pr_r * bt_r - pr_i * bt_i
    w_i = pr_r * bt_i + pr_i * bt_r
    w_ref[0] = jnp.concatenate([w_r, w_i], axis=0).astype(BF16)


def _toeplitz_kernel(kt_ref, mt_ref, *, group):
    tau = pl.program_id(0)
    g, _, n = kt_ref.shape
    row = lax.broadcasted_iota(jnp.int32, (n, n), 0)
    col = lax.broadcasted_iota(jnp.int32, (n, n), 1)
    shift = (col == row + group * tau).astype(BF16)
    kt = kt_ref[...].reshape(g * group, n).astype(BF16)
    res = jnp.dot(kt, shift, preferred_element_type=F32)
    mt_ref[...] = res.reshape(g, group, n).astype(BF16)


def _ssm_main_kernel(x_ref, mt_ref, w_ref, vt_ref, ar_ref, ai_ref, y_ref, *, n_chunks, n_steps):
    x = x_ref[0]
    y = jnp.dot(x, mt_ref[0], preferred_element_type=F32)
    s = lax.dot_general(x, w_ref[0], (((1,), (1,)), ((), ())), preferred_element_type=F32)
    half = s.shape[1] // 2
    chunk = lax.broadcasted_iota(jnp.int32, s.shape, 0) % n_chunks
    for j in range(n_steps):
        k = 1 << j
        sh = jnp.where(chunk >= k, pltpu.roll(s, k, 0), 0.0)
        s = s + ar_ref[0, j:j + 1, :] * sh + ai_ref[0, j:j + 1, :] * pltpu.roll(sh, half, 1)
    s_in = jnp.where(chunk >= 1, pltpu.roll(s, 1, 0), 0.0)
    y_ref[0] = y + jnp.dot(s_in.astype(BF16), vt_ref[0], preferred_element_type=F32)


def _ssm_conv(u, batch, log_dt, a_re, a_im, b_re, b_im, c_re, c_im):
    t, d_ssm = u.shape
    g, p = a_re.shape
    h = b_re.shape[-1]
    lc = SSM_CHUNK
    n = lc * h
    seq = t // batch
    n_chunks = seq // lc
    rows = batch * n_chunks
    n_steps = max(1, (n_chunks - 1).bit_length())

    lb_re, lb_im, bb_re, bb_im = _ssm_discretise(log_dt, a_re, a_im, b_re, b_im)
    pw_r, pw_i = [jnp.ones_like(lb_re)], [jnp.zeros_like(lb_re)]
    for _ in range(lc):
        pw_r, pw_i = (pw_r + [pw_r[-1] * lb_re - pw_i[-1] * lb_im],
                      pw_i + [pw_r[-1] * lb_im + pw_i[-1] * lb_re])
    pw_r = jnp.stack(pw_r, axis=-1)
    pw_i = jnp.stack(pw_i, axis=-1)
    lane_rep = lambda a: jnp.repeat(a, h, axis=-1)
    lane_tile = lambda a: jnp.tile(a, (1, 1, lc))
    tables = [
        lane_rep(pw_r[..., :lc]), lane_rep(pw_i[..., :lc]),
        lane_rep(pw_r[..., lc - 1::-1]), lane_rep(pw_i[..., lc - 1::-1]),
        lane_tile(c_re.transpose(0, 2, 1)), lane_tile(c_im.transpose(0, 2, 1)),
        lane_tile(bb_re), lane_tile(bb_im),
    ]
    sr, si = pw_r[..., lc], pw_i[..., lc]
    ar, ai = [], []
    for _ in range(n_steps):
        ar.append(jnp.concatenate([sr, sr], axis=-1))
        ai.append(jnp.concatenate([-si, si], axis=-1))
        sr, si = sr * sr - si * si, 2.0 * sr * si
    ar = jnp.stack(ar, axis=1)
    ai = jnp.stack(ai, axis=1)

    tab_spec = pl.BlockSpec((1, p, n), lambda i: (i, 0, 0))
    kt, w, vt = pl.pallas_call(
        _ssm_ops_kernel,
        grid=(g,),
        in_specs=[tab_spec] * 8 + [pl.BlockSpec((1, h, p), lambda i: (i, 0, 0))] * 2
        + [pl.BlockSpec((1, p, 1), lambda i: (i, 0, 0))] * 2,
        out_specs=[pl.BlockSpec((1, h, n), lambda i: (i, 0, 0)),
                   pl.BlockSpec((1, 2 * p, n), lambda i: (i, 0, 0)),
                   pl.BlockSpec((1, 2 * p, n), lambda i: (i, 0, 0))],
        out_shape=[jax.ShapeDtypeStruct((g, h, n), F32),
                   jax.ShapeDtypeStruct((g, 2 * p, n), BF16),
                   jax.ShapeDtypeStruct((g, 2 * p, n), BF16)],
        compiler_params=_params("parallel"),
        name="ssm_ops",
    )(*tables, bb_re.transpose(0, 2, 1), bb_im.transpose(0, 2, 1), lb_re[..., None], lb_im[..., None])

    mt = pl.pallas_call(
        functools.partial(_toeplitz_kernel, group=h),
        grid=(lc,),
        in_specs=[pl.BlockSpec((g, h, n), lambda i: (0, 0, 0))],
        out_specs=pl.BlockSpec((g, None, h, n), lambda i: (0, i, 0, 0)),
        out_shape=jax.ShapeDtypeStruct((g, lc, h, n), BF16),
        compiler_params=_params("parallel"),
        name="ssm_toeplitz",
    )(kt).reshape(g, n, n)

    x = u.astype(BF16).reshape(rows, lc, g, h).transpose(2, 0, 1, 3).reshape(g, rows, n)
    y = pl.pallas_call(
        functools.partial(_ssm_main_kernel, n_chunks=n_chunks, n_steps=n_steps),
        grid=(g,),
        in_specs=[pl.BlockSpec((1, rows, n), lambda i: (i, 0, 0)),
                  pl.BlockSpec((1, n, n), lambda i: (i, 0, 0)),
                  pl.BlockSpec((1, 2 * p, n), lambda i: (i, 0, 0)),
                  pl.BlockSpec((1, 2 * p, n), lambda i: (i, 0, 0)),
                  pl.BlockSpec((1, n_steps, 2 * p), lambda i: (i, 0, 0)),
                  pl.BlockSpec((1, n_steps, 2 * p), lambda i: (i, 0, 0))],
        out_specs=pl.BlockSpec((1, rows, n), lambda i: (i, 0, 0)),
        out_shape=jax.ShapeDtypeStruct((g, rows, n), F32),
        compiler_params=_params("parallel"),
        name="ssm_main",
    )(x, mt, w, vt, ar, ai)
    return y.reshape(g, rows, lc, h).transpose(1, 2, 0, 3).reshape(t, d_ssm)


def _regroup(n, dil, hop):
    rows = n // (dil * hop)
    for res in range(dil * hop):
        a, r = divmod(res, dil)
        yield slice(res * rows, (res + 1) * rows), pl.ds(r * (n // dil) + a, rows, stride=hop)


def _attn_kernel(q_ref, k_ref, v_ref, o_ref, stage, kp4, vp4, kp16, vp16, qf, qp, ob, mb, lb, *, span, blk, unroll):
    s = pl.program_id(2)
    seq, lanes = k_ref.shape[1], k_ref.shape[2]
    dils = [d for _, d in DILATIONS]
    hop = dils[1]
    k_perm = {dils[1]: kp4, dils[2]: kp16}
    v_perm = {dils[1]: vp4, dils[2]: vp16}

    @pl.when(s == 0)
    def _():
        for src, perm in ((k_ref, k_perm), (v_ref, v_perm)):
            stage[0] = src[0].astype(F32)
            for dst, rows in _regroup(seq, dils[0], hop):
                x = stage[0, rows, :]
                stage[1, dst, :] = x
                perm[dils[1]][dst, :] = x.astype(BF16)
            for dst, rows in _regroup(seq, dils[1], hop):
                perm[dils[2]][dst, :] = stage[1, rows, :].astype(BF16)

    scale = 1.0 / (HEAD_DIM ** 0.5)
    q0 = q_ref[0].astype(F32) * scale
    qf[0] = q0
    qp[0] = q0.astype(BF16)
    for dst, rows in _regroup(span, dils[0], hop):
        x = qf[0, rows, :]
        qf[1, dst, :] = x
        qp[1, dst, :] = x.astype(BF16)
    for dst, rows in _regroup(span, dils[1], hop):
        qp[2, dst, :] = qf[1, rows, :].astype(BF16)

    head0 = lax.broadcasted_iota(jnp.int32, (blk, lanes), 1) < HEAD_DIM
    head0_kv = lax.broadcasted_iota(jnp.int32, (2 * blk, lanes), 1) < HEAD_DIM
    rel =(lax.broadcasted_iota(jnp.int32, (blk, 2 * blk), 0)
           - lax.broadcasted_iota(jnp.int32, (blk, 2 * blk), 1))

    for di, dil in enumerate(dils):
        per_res = span // dil
        n_blk = per_res // blk
        res_len = seq // dil

        def body(it, carry, di=di, dil=dil, per_res=per_res, n_blk=n_blk, res_len=res_len):
            res = it // n_blk
            jj = it % n_blk
            qb = s * per_res + jj * blk
            kb = jnp.maximum(qb - blk, 0)
            row0 = pl.multiple_of(res * per_res + jj * blk, blk)
            koff = pl.multiple_of(res * res_len + kb, blk)
            q = qp[di, pl.ds(row0, blk), :]
            if dil == 1:
                kk = k_ref[0, pl.ds(koff, 2 * blk), :]
                vv = v_ref[0, pl.ds(koff, 2 * blk), :]
            else:
                kk = k_perm[dil][pl.ds(koff, 2 * blk), :]
                vv = v_perm[dil][pl.ds(koff, 2 * blk), :]
            lo = kb - qb
            valid = (rel >= lo) & (rel <= lo + blk)
            outs, maxs = [], []
            for head_mask, kv_mask in ((head0, head0_kv), (~head0, ~head0_kv)):
                qh = jnp.where(head_mask, q, jnp.zeros_like(q))
                sc = lax.dot_general(qh, kk, (((1,), (1,)), ((), ())), preferred_element_type=F32)
                sc = jnp.where(valid, sc, NEG_INF)
                m = jnp.max(sc, axis=-1, keepdims=True)
                p = jnp.exp(sc - m)
                maxs.append(m)
                outs.append(jnp.dot(p.astype(BF16), jnp.where(kv_mask, vv, jnp.ones_like(vv)),
                                    preferred_element_type=F32))
            ob[di, pl.ds(row0, blk), :] = jnp.where(head0, outs[0], outs[1])
            mb[di, pl.ds(row0, blk), :] = jnp.where(head0, maxs[0], maxs[1])
            lb[di, pl.ds(row0, blk), :] = jnp.where(head0, outs[1], outs[0])
            return carry

        lax.fori_loop(0, dil * n_blk, body, 0, unroll=unroll)

    unswap = lambda x: pltpu.roll(x, HEAD_DIM, 1)
    for di in range(1, len(dils)):
        for dst, rows in _regroup(span, dils[di - 1], hop):
            m_a = mb[di - 1, rows, :]
            l_a = lb[di - 1, rows, :]
            if di == 1:
                l_a = unswap(l_a)
            m_b = mb[di, dst, :]
            m_new = jnp.maximum(m_a, m_b)
            w_a = jnp.exp(m_a - m_new)
            w_b = jnp.exp(m_b - m_new)
            ob[di, dst, :] = ob[di - 1, rows, :] * w_a + ob[di, dst, :] * w_b
            lb[di, dst, :] = l_a * w_a + unswap(lb[di, dst, :]) * w_b
            mb[di, dst, :] = m_new
    last = len(dils) - 1
    for di in range(last, 0, -1):
        for dst, rows in _regroup(span, dils[di - 1], hop):
            x = ob[di, dst, :]
            ob[di - 1, rows, :] = x / lb[di, dst, :] if di == last else x
    o_ref[0] = ob[0].astype(o_ref.dtype)


def _attention(q, k, v, batch, *, unroll=8):
    t, c = q.shape
    seq = t // batch
    span, blk = ATTN_SPAN, ATTN_BLOCK
    dils = [d for _, d in DILATIONS]
    assert dils == [1, 4, 16] and all(w // d == blk and span % (blk * d) == 0 for w, d in DILATIONS)
    assert seq % span == 0 and V7X_LANES % HEAD_DIM == 0
    q3, k3, v3 = (a.reshape(batch, seq, c) for a in (q, k, v))
    lanes = V7X_LANES
    full = pl.BlockSpec((1, seq, lanes), lambda b, hp, s: (b, 0, hp))
    tile = pl.BlockSpec((1, span, lanes), lambda b, hp, s: (b, s, hp))
    out = pl.pallas_call(
        functools.partial(_attn_kernel, span=span, blk=blk, unroll=unroll),
        grid=(batch, c // lanes, seq // span),
        in_specs=[tile, full, full],
        out_specs=tile,
        out_shape=jax.ShapeDtypeStruct((batch, seq, c), BF16),
        scratch_shapes=[pltpu.VMEM((2, seq, lanes), F32)]
        + [pltpu.VMEM((seq, lanes), BF16)] * 4
        + [pltpu.VMEM((2, span, lanes), F32), pltpu.VMEM((len(dils), span, lanes), BF16)]
        + [pltpu.VMEM((len(dils), span, lanes), F32)] * 3,
        compiler_params=_params("parallel", "parallel", "arbitrary"),
        name="attn",
    )(q3, k3, v3)
    return out.reshape(t, c)


def _post_kernel(yc_ref, u_ref, att_ref, h_ref, d_ref, wglu_ref, bglu_ref, gs_ref, ga_ref, wo_ref, o_ref):
    y = jax.nn.gelu(yc_ref[...] + d_ref[...] * u_ref[...])
    gate = jax.nn.sigmoid(jnp.dot(y.astype(BF16), wglu_ref[...], preferred_element_type=F32) + bglu_ref[...])
    ys = _rmsnorm(y * gate, gs_ref[...]).astype(BF16)
    ya = _rmsnorm(att_ref[...].astype(F32), ga_ref[...]).astype(BF16)
    d_ssm = ys.shape[1]
    o_ref[...] = (h_ref[...]
                  + jnp.dot(ys, wo_ref[:d_ssm, :], preferred_element_type=F32)
                  + jnp.dot(ya, wo_ref[d_ssm:, :], preferred_element_type=F32))


def _post(yc, u, att, h, d, w_glu, b_glu, g_ssm, g_att, w_out, *, tm=512):
    t, dm = h.shape
    c = yc.shape[1]
    row = lambda i: (i, 0)
    const = lambda i: (0, 0)
    return pl.pallas_call(
        _post_kernel,
        grid=(t // tm,),
        in_specs=[pl.BlockSpec((tm, c), row), pl.BlockSpec((tm, c), row), pl.BlockSpec((tm, c), row),
                  pl.BlockSpec((tm, dm), row),
                  pl.BlockSpec((1, c), const), pl.BlockSpec((c, c), const), pl.BlockSpec((1, c), const),
                  pl.BlockSpec((1, c), const), pl.BlockSpec((1, c), const), pl.BlockSpec((2 * c, dm), const)],
        out_specs=pl.BlockSpec((tm, dm), row),
        out_shape=jax.ShapeDtypeStruct((t, dm), F32),
        compiler_params=_params("parallel"),
        name="post",
    )(yc, u, att, h, d.reshape(1, c), w_glu, b_glu.reshape(1, c), g_ssm.reshape(1, c), g_att.reshape(1, c), w_out)


def _pad_axis(w, axis, mult):
    pad = [(0, 0), (0, 0)]
    pad[axis] = (0, (-w.shape[axis]) % mult)
    return jnp.pad(w, pad)


def kernel(x, ffn1_norm, ffn1_w_gate, ffn1_w_up, ffn1_w_down, mix_norm, w_in, ssm_log_dt, ssm_a_re, ssm_a_im, ssm_b_re, ssm_b_im, ssm_c_re, ssm_c_im, ssm_d, ssm_w_glu, ssm_b_glu, ssm_out_norm, attn_out_norm, w_out, ffn2_norm, ffn2_w_gate, ffn2_w_up, ffn2_w_down, final_norm):
    batch, seq, d_model = x.shape
    depth = ffn1_norm.shape[0]
    tf = 512
    h = x.reshape(batch * seq, d_model)
    for l in range(depth):
        last = l == depth - 1
        ffn_w = lambda wg, wu, wd: (_pad_axis(wg.astype(BF16), 1, tf), _pad_axis(wu.astype(BF16), 1, tf),
                                    _pad_axis(wd.astype(BF16), 0, tf))
        h = _ffn(h, ffn1_norm[l], *ffn_w(ffn1_w_gate[l], ffn1_w_up[l], ffn1_w_down[l]), tf=tf)
        q, k, v, u = _inproj(h, mix_norm[l], w_in[l].astype(BF16))
        yc = _ssm_conv(u, batch, ssm_log_dt[l], ssm_a_re[l], ssm_a_im[l], ssm_b_re[l], ssm_b_im[l],
                       ssm_c_re[l], ssm_c_im[l])
        att = _attention(q, k, v, batch)
        h = _post(yc, u, att, h, ssm_d[l], ssm_w_glu[l].astype(BF16), ssm_b_glu[l], ssm_out_norm[l],
                  attn_out_norm[l], w_out[l].astype(BF16))
        h = _ffn(h, ffn2_norm[l], *ffn_w(ffn2_w_gate[l], ffn2_w_up[l], ffn2_w_down[l]),
                 final_g=final_norm if last else None, tf=tf)
    return h.reshape(batch, seq, d_model)
```

```python
import functools

import jax
import jax.numpy as jnp
from jax import lax
from jax.experimental import pallas as pl
from jax.experimental.pallas import tpu as pltpu

F32 = jnp.float32
BF16 = jnp.bfloat16

EPS = 1e-6
NEG_INF = -1e30
HEAD_DIM = 64
SSM_GROUP = 16
DILATIONS = ((128, 1), (512, 4), (2048, 16))

V7X_LANES = 128
V7X_VMEM_BYTES = 64 * 1024 * 1024
VMEM_LIMIT_BYTES = V7X_VMEM_BYTES - 6 * 1024 * 1024

SSM_CHUNK = 32
ATTN_SPAN = 2048
ATTN_BLOCK = 128


def _rmsnorm(x, g):
    return x * lax.rsqrt(jnp.mean(x * x, axis=-1, keepdims=True) + EPS) * g


def _params(*semantics):
    return pltpu.CompilerParams(dimension_semantics=semantics, vmem_limit_bytes=VMEM_LIMIT_BYTES)


def _ffn_kernel(*refs, n_f, n_chunks, overlap, final_norm):
    if final_norm:
        h_ref, g_ref, wg_ref, wu_ref, wd_ref, fg_ref, o_ref, n_ref = refs
    else:
        h_ref, g_ref, wg_ref, wu_ref, wd_ref, o_ref, n_ref = refs
    j = pl.program_id(1)

    @pl.when(j == 0)
    def _():
        h = h_ref[...]
        n_ref[...] = _rmsnorm(h, g_ref[...]).astype(BF16)
        o_ref[...] = h

    n = n_ref[...]
    gate = jnp.dot(n, wg_ref[...], preferred_element_type=F32)
    up = jnp.dot(n, wu_ref[...], preferred_element_type=F32)
    a = 0.5 * (gate * jax.nn.sigmoid(gate)) * up
    if overlap:
        col = lax.broadcasted_iota(jnp.int32, a.shape, 1)
        a = jnp.where(col >= jnp.where(j == n_f - 1, overlap, 0), a, 0.0)
    a = a.astype(BF16)
    cw = o_ref.shape[1] // n_chunks
    for c in range(n_chunks):
        sl = slice(c * cw, (c + 1) * cw)
        o_ref[:, sl] += jnp.dot(a, wd_ref[:, sl], preferred_element_type=F32)

    if final_norm:
        @pl.when(j == n_f - 1)
        def _():
            o_ref[...] = _rmsnorm(o_ref[...], fg_ref[...])


def _ffn(h, norm_g, wg, wu, wd, final_g=None, *, tm=1024, tf=512):
    t, d = h.shape
    f = wg.shape[1]
    n_f = pl.cdiv(f, tf)
    overlap = n_f * tf - f
    assert f >= tf and overlap % V7X_LANES == 0
    final_norm = final_g is not None
    row = lambda i, j: (i, 0)
    start = lambda j: pl.multiple_of(jnp.minimum(j * tf, f - tf), V7X_LANES)
    in_specs = [
        pl.BlockSpec((tm, d), row),
        pl.BlockSpec((1, d), lambda i, j: (0, 0)),
        pl.BlockSpec((pl.Element(d), pl.Element(tf)), lambda i, j: (0, start(j))),
        pl.BlockSpec((pl.Element(d), pl.Element(tf)), lambda i, j: (0, start(j))),
        pl.BlockSpec((pl.Element(tf), pl.Element(d)), lambda i, j: (start(j), 0)),
    ]
    args = [h, norm_g.reshape(1, d), wg, wu, wd]
    if final_norm:
        in_specs.append(pl.BlockSpec((1, d), lambda i, j: (0, 0)))
        args.append(final_g.reshape(1, d))
    return pl.pallas_call(
        functools.partial(_ffn_kernel, n_f=n_f, n_chunks=4, overlap=overlap, final_norm=final_norm),
        grid=(t // tm, n_f),
        in_specs=in_specs,
        out_specs=pl.BlockSpec((tm, d), row),
        out_shape=jax.ShapeDtypeStruct((t, d), F32),
        scratch_shapes=[pltpu.VMEM((tm, d), BF16)],
        compiler_params=_params("parallel", "arbitrary"),
        name="ffn_final" if final_norm else "ffn",
    )(*args)


def _inproj_kernel(h_ref, g_ref, w_ref, q_ref, k_ref, v_ref, u_ref, n_ref):
    j = pl.program_id(1)

    @pl.when(j == 0)
    def _():
        n_ref[...] = _rmsnorm(h_ref[...], g_ref[...]).astype(BF16)

    res = jnp.dot(n_ref[...], w_ref[...], preferred_element_type=F32)
    for idx, ref in enumerate((q_ref, k_ref, v_ref, u_ref)):
        @pl.when(j == idx)
        def _(ref=ref):
            ref[...] = res.astype(ref.dtype)


def _inproj(h, norm_g, w_in, *, tm=1024):
    t, d = h.shape
    c = w_in.shape[1] // 4
    row = lambda i, j: (i, 0)
    return pl.pallas_call(
        _inproj_kernel,
        grid=(t // tm, 4),
        in_specs=[
            pl.BlockSpec((tm, d), row),
            pl.BlockSpec((1, d), lambda i, j: (0, 0)),
            pl.BlockSpec((d, c), lambda i, j: (0, j)),
        ],
        out_specs=[pl.BlockSpec((tm, c), row)] * 4,
        out_shape=[jax.ShapeDtypeStruct((t, c), BF16)] * 3 + [jax.ShapeDtypeStruct((t, c), F32)],
        scratch_shapes=[pltpu.VMEM((tm, d), BF16)],
        compiler_params=_params("parallel", "arbitrary"),
        name="inproj",
    )(h, norm_g.reshape(1, d), w_in)


def _ssm_discretise(log_dt, a_re, a_im, b_re, b_im):
    dt = jnp.exp(log_dt)[:, None]
    mag = jnp.exp(a_re * dt)
    lb_re = mag * jnp.cos(a_im * dt)
    lb_im = mag * jnp.sin(a_im * dt)
    den = a_re * a_re + a_im * a_im
    nr = lb_re - 1.0
    ni = lb_im
    f_re = (nr * a_re + ni * a_im) / den
    f_im = (ni * a_re - nr * a_im) / den
    bb_re = f_re[..., None] * b_re - f_im[..., None] * b_im
    bb_im = f_re[..., None] * b_im + f_im[..., None] * b_re
    return lb_re, lb_im, bb_re, bb_im


def _ssm_ops_kernel(pwr, pwi, bba, bbb, cca, ccb, krev_ref, w_ref, v_ref, *, lc):
    bb_a, bb_b = bba[0], bbb[0]
    cc_a, cc_b = cca[0], ccb[0]
    cmul = lambda k, a, b: pwr[0, k:k + 1, :] * a + pwi[0, k:k + 1, :] * b
    w = jnp.concatenate([cmul(lc - 1 - tau, bb_a, bb_b) for tau in range(lc)], axis=0)
    v = jnp.concatenate([cmul(t + 1, cc_a, cc_b) for t in range(lc)], axis=0)
    w_ref[0] = w.astype(BF16)
    v_ref[0] = v.astype(BF16)
    krev_ref[0] = lax.dot_general(cc_a, w, (((1,), (1,)), ((), ())), precision=lax.Precision.HIGHEST,
                                  preferred_element_type=F32)


def _toeplitz_kernel(krev_ref, m_ref, *, group, lc):
    t = pl.program_id(0)
    g, _, n = krev_ref.shape
    row = lax.broadcasted_iota(jnp.int32, (n, n), 0)
    col = lax.broadcasted_iota(jnp.int32, (n, n), 1)
    shift = (row == col + group * (lc - 1 - t)).astype(BF16)
    krev = krev_ref[...].reshape(g * group, n).astype(BF16)
    res = jnp.dot(krev, shift, preferred_element_type=F32)
    m_ref[...] = res.reshape(g, group, n).astype(BF16)


def _ssm_pack_kernel(u_ref, o_ref, *, lc):
    n_chunks = u_ref.shape[0] // lc
    for tau in range(lc):
        x = u_ref[pl.ds(tau, n_chunks, stride=lc), :]
        o_ref[:, tau * n_chunks:(tau + 1) * n_chunks] = x.T.astype(o_ref.dtype)


def _ssm_unpack_kernel(y_ref, o_ref, *, lc):
    n_chunks = o_ref.shape[0] // lc
    for t in range(lc):
        o_ref[pl.ds(t, n_chunks, stride=lc), :] = y_ref[:, t * n_chunks:(t + 1) * n_chunks].T


def _ssm_main_kernel(ut_ref, m_ref, w_ref, v_ref, ar_ref, ai_ref, yt_ref, *, batch, lc, pack, n_steps):
    group = ut_ref.shape[0]
    seq = ut_ref.shape[1] // batch
    blocks = seq // (pack * lc)
    n_chunks = blocks * pack
    half = w_ref.shape[2] // 2
    chunk = lax.broadcasted_iota(jnp.int32, (n_chunks, 2 * half), 0)
    for b in range(batch):
        lane0 = lambda blk, k: b * seq + blk * (pack * lc) + k * pack
        xt = jnp.concatenate(
            [jnp.concatenate([ut_ref[:, pl.ds(lane0(blk, tau), pack)] for blk in range(blocks)], axis=1)
             for tau in range(lc)], axis=0)
        yt = jnp.dot(m_ref[0], xt, preferred_element_type=F32)
        s = lax.dot_general(xt, w_ref[0], (((0,), (0,)), ((), ())), preferred_element_type=F32)
        for j in range(n_steps):
            k = 1 << j
            sh = jnp.where(chunk >= k, pltpu.roll(s, k, 0), 0.0)
            s = s + ar_ref[0, j:j + 1, :] * sh + ai_ref[0, j:j + 1, :] * pltpu.roll(sh, half, 1)
        s_in = jnp.where(chunk >= 1, pltpu.roll(s, 1, 0), 0.0)
        yt = yt + lax.dot_general(v_ref[0], s_in.astype(BF16), (((1,), (1,)), ((), ())),
                                  preferred_element_type=F32)
        for t in range(lc):
            for blk in range(blocks):
                yt_ref[:, pl.ds(lane0(blk, t), pack)] = yt[t * group:(t + 1) * group, blk * pack:(blk + 1) * pack]


def _ssm_conv(u, batch, log_dt, a_re, a_im, b_re, b_im, c_re, c_im):
    t, d_ssm = u.shape
    g, p = a_re.shape
    h = b_re.shape[-1]
    lc = SSM_CHUNK
    n = lc * h
    seq = t // batch
    pack = V7X_LANES
    pb = pack * lc
    n_steps = max(1, (seq // lc - 1).bit_length())
    assert seq % pb == 0 and d_ssm % V7X_LANES == 0

    lb_re, lb_im, bb_re, bb_im = _ssm_discretise(log_dt, a_re, a_im, b_re, b_im)
    pw_r, pw_i = [jnp.ones_like(lb_re)], [jnp.zeros_like(lb_re)]
    for _ in range(lc):
        pw_r, pw_i = (pw_r + [pw_r[-1] * lb_re - pw_i[-1] * lb_im],
                      pw_i + [pw_r[-1] * lb_im + pw_i[-1] * lb_re])
    pw_r = jnp.stack(pw_r, axis=1)
    pw_i = jnp.stack(pw_i, axis=1)
    pair = lambda a, b: jnp.concatenate([a, b], axis=-1)
    bt_r, bt_i = bb_re.transpose(0, 2, 1), bb_im.transpose(0, 2, 1)
    tables = [pair(pw_r, pw_r), pair(pw_i, pw_i), pair(bt_r, bt_i), pair(-bt_i, bt_r),
              pair(c_re, -c_im), pair(-c_im, -c_re)]
    sr, si = pw_r[:, lc], pw_i[:, lc]
    ar, ai = [], []
    for _ in range(n_steps):
        ar.append(pair(sr, sr))
        ai.append(pair(-si, si))
        sr, si = sr * sr - si * si, 2.0 * sr * si
    ar = jnp.stack(ar, axis=1)
    ai = jnp.stack(ai, axis=1)

    per_group = lambda *shape: pl.BlockSpec((1,) + shape, lambda i: (i,) + (0,) * len(shape))
    krev, w, v = pl.pallas_call(
        functools.partial(_ssm_ops_kernel, lc=lc),
        grid=(g,),
        in_specs=[per_group(lc + 1, 2 * p)] * 2 + [per_group(h, 2 * p)] * 4,
        out_specs=[per_group(h, n), per_group(n, 2 * p), per_group(n, 2 * p)],
        out_shape=[jax.ShapeDtypeStruct((g, h, n), F32),
                   jax.ShapeDtypeStruct((g, n, 2 * p), BF16),
                   jax.ShapeDtypeStruct((g, n, 2 * p), BF16)],
        compiler_params=_params("parallel"),
        name="ssm_ops",
    )(*tables)

    m = pl.pallas_call(
        functools.partial(_toeplitz_kernel, group=h, lc=lc),
        grid=(lc,),
        in_specs=[pl.BlockSpec((g, h, n), lambda i: (0, 0, 0))],
        out_specs=pl.BlockSpec((g, None, h, n), lambda i: (0, i, 0, 0)),
        out_shape=jax.ShapeDtypeStruct((g, lc, h, n), BF16),
        compiler_params=_params("parallel"),
        name="ssm_toeplitz",
    )(krev).reshape(g, n, n)

    tok_blk = pl.BlockSpec((pb, V7X_LANES), lambda i, j: (i, j))
    ch_blk = pl.BlockSpec((V7X_LANES, pb), lambda i, j: (j, i))
    relayout_grid = (t // pb, d_ssm // V7X_LANES)
    ut = pl.pallas_call(
        functools.partial(_ssm_pack_kernel, lc=lc),
        grid=relayout_grid,
        in_specs=[tok_blk],
        out_specs=ch_blk,
        out_shape=jax.ShapeDtypeStruct((d_ssm, t), BF16),
        compiler_params=_params("parallel", "parallel"),
        name="ssm_pack",
    )(u)
    yt = pl.pallas_call(
        functools.partial(_ssm_main_kernel, batch=batch, lc=lc, pack=pack, n_steps=n_steps),
        grid=(g,),
        in_specs=[pl.BlockSpec((h, t), lambda i: (i, 0)), per_group(n, n), per_group(n, 2 * p), per_group(n, 2 * p),
                  per_group(n_steps, 2 * p), per_group(n_steps, 2 * p)],
        out_specs=pl.BlockSpec((h, t), lambda i: (i, 0)),
        out_shape=jax.ShapeDtypeStruct((d_ssm, t), F32),
        compiler_params=_params("parallel"),
        name="ssm_main",
    )(ut, m, w, v, ar, ai)
    return pl.pallas_call(
        functools.partial(_ssm_unpack_kernel, lc=lc),
        grid=relayout_grid,
        in_specs=[ch_blk],
        out_specs=tok_blk,
        out_shape=jax.ShapeDtypeStruct((t, d_ssm), F32),
        compiler_params=_params("parallel", "parallel"),
        name="ssm_unpack",
    )(yt)


def _regroup(n, dil, hop):
    rows = n // (dil * hop)
    for res in range(dil * hop):
        a, r = divmod(res, dil)
        yield slice(res * rows, (res + 1) * rows), pl.ds(r * (n // dil) + a, rows, stride=hop)


def _attn_kernel(q_ref, k_ref, v_ref, o_ref, stage, kp4, vp4, kp16, vp16, qf, qp, ob, mb, lb, *, span, blk, unroll):
    s = pl.program_id(2)
    seq, lanes = k_ref.shape[1], k_ref.shape[2]
    dils = [d for _, d in DILATIONS]
    hop = dils[1]
    k_perm = {dils[1]: kp4, dils[2]: kp16}
    v_perm = {dils[1]: vp4, dils[2]: vp16}

    @pl.when(s == 0)
    def _():
        for src, perm in ((k_ref, k_perm), (v_ref, v_perm)):
            stage[0] = src[0].astype(F32)
            for dst, rows in _regroup(seq, dils[0], hop):
                x = stage[0, rows, :]
                stage[1, dst, :] = x
                perm[dils[1]][dst, :] = x.astype(BF16)
            for dst, rows in _regroup(seq, dils[1], hop):
                perm[dils[2]][dst, :] = stage[1, rows, :].astype(BF16)

    scale = 1.0 / (HEAD_DIM ** 0.5)
    q0 = q_ref[0].astype(F32) * scale
    qf[0] = q0
    qp[0] = q0.astype(BF16)
    for dst, rows in _regroup(span, dils[0], hop):
        x = qf[0, rows, :]
        qf[1, dst, :] = x
        qp[1, dst, :] = x.astype(BF16)
    for dst, rows in _regroup(span, dils[1], hop):
        qp[2, dst, :] = qf[1, rows, :].astype(BF16)

    head0 = lax.broadcasted_iota(jnp.int32, (blk, lanes), 1) < HEAD_DIM
    head0_kv = lax.broadcasted_iota(jnp.int32, (2 * blk, lanes), 1) < HEAD_DIM
    rel =(lax.broadcasted_iota(jnp.int32, (blk, 2 * blk), 0)
           - lax.broadcasted_iota(jnp.int32, (blk, 2 * blk), 1))

    for di, dil in enumerate(dils):
        per_res = span // dil
        n_blk = per_res // blk
        res_len = seq // dil

        def body(it, carry, di=di, dil=dil, per_res=per_res, n_blk=n_blk, res_len=res_len):
            res = it // n_blk
            jj = it % n_blk
            qb = s * per_res + jj * blk
            kb = jnp.maximum(qb - blk, 0)
            row0 = pl.multiple_of(res * per_res + jj * blk, blk)
            koff = pl.multiple_of(res * res_len + kb, blk)
            q = qp[di, pl.ds(row0, blk), :]
            if dil == 1:
                kk = k_ref[0, pl.ds(koff, 2 * blk), :]
                vv = v_ref[0, pl.ds(koff, 2 * blk), :]
            else:
                kk = k_perm[dil][pl.ds(koff, 2 * blk), :]
                vv = v_perm[dil][pl.ds(koff, 2 * blk), :]
            lo = kb - qb
            valid = (rel >= lo) & (rel <= lo + blk)
            outs, maxs = [], []
            for head_mask, kv_mask in ((head0, head0_kv), (~head0, ~head0_kv)):
                qh = jnp.where(head_mask, q, jnp.zeros_like(q))
                sc = lax.dot_general(qh, kk, (((1,), (1,)), ((), ())), preferred_element_type=F32)
                sc = jnp.where(valid, sc, NEG_INF)
                m = jnp.max(sc, axis=-1, keepdims=True)
                p = jnp.exp(sc - m)
                maxs.append(m)
                outs.append(jnp.dot(p.astype(BF16), jnp.where(kv_mask, vv, jnp.ones_like(vv)),
                                    preferred_element_type=F32))
            ob[di, pl.ds(row0, blk), :] = jnp.where(head0, outs[0], outs[1])
            mb[di, pl.ds(row0, blk), :] = jnp.where(head0, maxs[0], maxs[1])
            lb[di, pl.ds(row0, blk), :] = jnp.where(head0, outs[1], outs[0])
            return carry

        lax.fori_loop(0, dil * n_blk, body, 0, unroll=unroll)

    unswap = lambda x: pltpu.roll(x, HEAD_DIM, 1)
    for di in range(1, len(dils)):
        for dst, rows in _regroup(span, dils[di - 1], hop):
            m_a = mb[di - 1, rows, :]
            l_a = lb[di - 1, rows, :]
            if di == 1:
                l_a = unswap(l_a)
            m_b = mb[di, dst, :]
            m_new = jnp.maximum(m_a, m_b)
            w_a = jnp.exp(m_a - m_new)
            w_b = jnp.exp(m_b - m_new)
            ob[di, dst, :] = ob[di - 1, rows, :] * w_a + ob[di, dst, :] * w_b
            lb[di, dst, :] = l_a * w_a + unswap(lb[di, dst, :]) * w_b
            mb[di, dst, :] = m_new
    last = len(dils) - 1
    for di in range(last, 0, -1):
        for dst, rows in _regroup(span, dils[di - 1], hop):
            x = ob[di, dst, :]
            ob[di - 1, rows, :] = x / lb[di, dst, :] if di == last else x
    o_ref[0] = ob[0].astype(o_ref.dtype)


def _attention(q, k, v, batch, *, unroll=8):
    t, c = q.shape
    seq = t // batch
    span, blk = ATTN_SPAN, ATTN_BLOCK
    dils = [d for _, d in DILATIONS]
    assert dils == [1, 4, 16] and all(w // d == blk and span % (blk * d) == 0 for w, d in DILATIONS)
    assert seq % span == 0 and V7X_LANES % HEAD_DIM == 0
    q3, k3, v3 = (a.reshape(batch, seq, c) for a in (q, k, v))
    lanes = V7X_LANES
    full = pl.BlockSpec((1, seq, lanes), lambda b, hp, s: (b, 0, hp))
    tile = pl.BlockSpec((1, span, lanes), lambda b, hp, s: (b, s, hp))
    out = pl.pallas_call(
        functools.partial(_attn_kernel, span=span, blk=blk, unroll=unroll),
        grid=(batch, c // lanes, seq // span),
        in_specs=[tile, full, full],
        out_specs=tile,
        out_shape=jax.ShapeDtypeStruct((batch, seq, c), BF16),
        scratch_shapes=[pltpu.VMEM((2, seq, lanes), F32)]
        + [pltpu.VMEM((seq, lanes), BF16)] * 4
        + [pltpu.VMEM((2, span, lanes), F32), pltpu.VMEM((len(dils), span, lanes), BF16)]
        + [pltpu.VMEM((len(dils), span, lanes), F32)] * 3,
        compiler_params=_params("parallel", "parallel", "arbitrary"),
        name="attn",
    )(q3, k3, v3)
    return out.reshape(t, c)


def _post_kernel(yc_ref, u_ref, att_ref, h_ref, d_ref, wglu_ref, bglu_ref, gs_ref, ga_ref, wo_ref, o_ref):
    y = jax.nn.gelu(yc_ref[...] + d_ref[...] * u_ref[...])
    gate = jax.nn.sigmoid(jnp.dot(y.astype(BF16), wglu_ref[...], preferred_element_type=F32) + bglu_ref[...])
    ys = _rmsnorm(y * gate, gs_ref[...]).astype(BF16)
    ya = _rmsnorm(att_ref[...].astype(F32), ga_ref[...]).astype(BF16)
    d_ssm = ys.shape[1]
    o_ref[...] = (h_ref[...]
                  + jnp.dot(ys, wo_ref[:d_ssm, :], preferred_element_type=F32)
                  + jnp.dot(ya, wo_ref[d_ssm:, :], preferred_element_type=F32))


def _post(yc, u, att, h, d, w_glu, b_glu, g_ssm, g_att, w_out, *, tm=512):
    t, dm = h.shape
    c = yc.shape[1]
    row = lambda i: (i, 0)
    const = lambda i: (0, 0)
    return pl.pallas_call(
        _post_kernel,
        grid=(t // tm,),
        in_specs=[pl.BlockSpec((tm, c), row), pl.BlockSpec((tm, c), row), pl.BlockSpec((tm, c), row),
                  pl.BlockSpec((tm, dm), row),
                  pl.BlockSpec((1, c), const), pl.BlockSpec((c, c), const), pl.BlockSpec((1, c), const),
                  pl.BlockSpec((1, c), const), pl.BlockSpec((1, c), const), pl.BlockSpec((2 * c, dm), const)],
        out_specs=pl.BlockSpec((tm, dm), row),
        out_shape=jax.ShapeDtypeStruct((t, dm), F32),
        compiler_params=_params("parallel"),
        name="post",
    )(yc, u, att, h, d.reshape(1, c), w_glu, b_glu.reshape(1, c), g_ssm.reshape(1, c), g_att.reshape(1, c), w_out)


def kernel(x, ffn1_norm, ffn1_w_gate, ffn1_w_up, ffn1_w_down, mix_norm, w_in, ssm_log_dt, ssm_a_re, ssm_a_im, ssm_b_re, ssm_b_im, ssm_c_re, ssm_c_im, ssm_d, ssm_w_glu, ssm_b_glu, ssm_out_norm, attn_out_norm, w_out, ffn2_norm, ffn2_w_gate, ffn2_w_up, ffn2_w_down, final_norm):
    batch, seq, d_model = x.shape
    depth = ffn1_norm.shape[0]
    h = x.reshape(batch * seq, d_model)
    for l in range(depth):
        last = l == depth - 1
        ffn_w = lambda *ws: [w.astype(BF16) for w in ws]
        h = _ffn(h, ffn1_norm[l], *ffn_w(ffn1_w_gate[l], ffn1_w_up[l], ffn1_w_down[l]))
        q, k, v, u = _inproj(h, mix_norm[l], w_in[l].astype(BF16))
        yc = _ssm_conv(u, batch, ssm_log_dt[l], ssm_a_re[l], ssm_a_im[l], ssm_b_re[l], ssm_b_im[l],
                       ssm_c_re[l], ssm_c_im[l])
        att = _attention(q, k, v, batch)
        h = _post(yc, u, att, h, ssm_d[l], ssm_w_glu[l].astype(BF16), ssm_b_glu[l], ssm_out_norm[l],
                  attn_out_norm[l], w_out[l].astype(BF16))
        h = _ffn(h, ffn2_norm[l], *ffn_w(ffn2_w_gate[l], ffn2_w_up[l], ffn2_w_down[l]),
                 final_g=final_norm if last else None)
    return h.reshape(batch, seq, d_model)
```

```python
import functools

import jax
import jax.numpy as jnp
from jax import lax
from jax.experimental import pallas as pl
from jax.experimental.pallas import tpu as pltpu

F32 = jnp.float32
BF16 = jnp.bfloat16

EPS = 1e-6
NEG_INF = -1e30
HEAD_DIM = 64
SSM_GROUP = 16
DILATIONS = ((128, 1), (512, 4), (2048, 16))

V7X_LANES = 128
V7X_VMEM_BYTES = 64 * 1024 * 1024
VMEM_LIMIT_BYTES = V7X_VMEM_BYTES - 6 * 1024 * 1024

SSM_CHUNK = 32
ATTN_SPAN = 2048
ATTN_BLOCK = 128


def _rmsnorm(x, g):
    return x * lax.rsqrt(jnp.mean(x * x, axis=-1, keepdims=True) + EPS) * g


def _params(*semantics):
    return pltpu.CompilerParams(dimension_semantics=semantics, vmem_limit_bytes=VMEM_LIMIT_BYTES)


def _ffn_kernel(*refs, n_f, n_chunks, overlap, final_norm):
    if final_norm:
        h_ref, g_ref, wg_ref, wu_ref, wd_ref, fg_ref, o_ref, n_ref = refs
    else:
        h_ref, g_ref, wg_ref, wu_ref, wd_ref, o_ref, n_ref = refs
    j = pl.program_id(1)

    @pl.when(j == 0)
    def _():
        h = h_ref[...]
        n_ref[...] = _rmsnorm(h, g_ref[...]).astype(BF16)
        o_ref[...] = h

    n = n_ref[...]
    gate = jnp.dot(n, wg_ref[...], preferred_element_type=F32)
    up = jnp.dot(n, wu_ref[...], preferred_element_type=F32)
    a = 0.5 * (gate * jax.nn.sigmoid(gate)) * up
    if overlap:
        col = lax.broadcasted_iota(jnp.int32, a.shape, 1)
        a = jnp.where(col >= jnp.where(j == n_f - 1, overlap, 0), a, 0.0)
    a = a.astype(BF16)
    cw = o_ref.shape[1] // n_chunks
    for c in range(n_chunks):
        sl = slice(c * cw, (c + 1) * cw)
        o_ref[:, sl] += jnp.dot(a, wd_ref[:, sl], preferred_element_type=F32)

    if final_norm:
        @pl.when(j == n_f - 1)
        def _():
            o_ref[...] = _rmsnorm(o_ref[...], fg_ref[...])


def _ffn(h, norm_g, wg, wu, wd, final_g=None, *, tm=1024, tf=512):
    t, d = h.shape
    f = wg.shape[1]
    n_f = pl.cdiv(f, tf)
    overlap = n_f * tf - f
    assert f >= tf and overlap % V7X_LANES == 0
    final_norm = final_g is not None
    row = lambda i, j: (i, 0)
    start = lambda j: pl.multiple_of(jnp.minimum(j * tf, f - tf), V7X_LANES)
    in_specs = [
        pl.BlockSpec((tm, d), row),
        pl.BlockSpec((1, d), lambda i, j: (0, 0)),
        pl.BlockSpec((pl.Element(d), pl.Element(tf)), lambda i, j: (0, start(j))),
        pl.BlockSpec((pl.Element(d), pl.Element(tf)), lambda i, j: (0, start(j))),
        pl.BlockSpec((pl.Element(tf), pl.Element(d)), lambda i, j: (start(j), 0)),
    ]
    args = [h, norm_g.reshape(1, d), wg, wu, wd]
    if final_norm:
        in_specs.append(pl.BlockSpec((1, d), lambda i, j: (0, 0)))
        args.append(final_g.reshape(1, d))
    return pl.pallas_call(
        functools.partial(_ffn_kernel, n_f=n_f, n_chunks=4, overlap=overlap, final_norm=final_norm),
        grid=(t // tm, n_f),
        in_specs=in_specs,
        out_specs=pl.BlockSpec((tm, d), row),
        out_shape=jax.ShapeDtypeStruct((t, d), F32),
        scratch_shapes=[pltpu.VMEM((tm, d), BF16)],
        compiler_params=_params("parallel", "arbitrary"),
        name="ffn_final" if final_norm else "ffn",
    )(*args)


def _inproj_kernel(h_ref, g_ref, w_ref, q_ref, k_ref, v_ref, u_ref, n_ref):
    j = pl.program_id(1)

    @pl.when(j == 0)
    def _():
        n_ref[...] = _rmsnorm(h_ref[...], g_ref[...]).astype(BF16)

    res = jnp.dot(n_ref[...], w_ref[...], preferred_element_type=F32)
    for idx, ref in enumerate((q_ref, k_ref, v_ref, u_ref)):
        @pl.when(j == idx)
        def _(ref=ref):
            ref[...] = res.astype(ref.dtype)


def _inproj(h, norm_g, w_in, *, tm=1024):
    t, d = h.shape
    c = w_in.shape[1] // 4
    row = lambda i, j: (i, 0)
    return pl.pallas_call(
        _inproj_kernel,
        grid=(t // tm, 4),
        in_specs=[
            pl.BlockSpec((tm, d), row),
            pl.BlockSpec((1, d), lambda i, j: (0, 0)),
            pl.BlockSpec((d, c), lambda i, j: (0, j)),
        ],
        out_specs=[pl.BlockSpec((tm, c), row)] * 4,
        out_shape=[jax.ShapeDtypeStruct((t, c), BF16)] * 3 + [jax.ShapeDtypeStruct((t, c), F32)],
        scratch_shapes=[pltpu.VMEM((tm, d), BF16)],
        compiler_params=_params("parallel", "arbitrary"),
        name="inproj",
    )(h, norm_g.reshape(1, d), w_in)


def _ssm_discretise(log_dt, a_re, a_im, b_re, b_im):
    dt = jnp.exp(log_dt)[:, None]
    mag = jnp.exp(a_re * dt)
    lb_re = mag * jnp.cos(a_im * dt)
    lb_im = mag * jnp.sin(a_im * dt)
    den = a_re * a_re + a_im * a_im
    nr = lb_re - 1.0
    ni = lb_im
    f_re = (nr * a_re + ni * a_im) / den
    f_im = (ni * a_re - nr * a_im) / den
    bb_re = f_re[..., None] * b_re - f_im[..., None] * b_im
    bb_im = f_re[..., None] * b_im + f_im[..., None] * b_re
    return lb_re, lb_im, bb_re, bb_im


def _ssm_ops_kernel(pwr, pwi, bba, bbb, cca, ccb, krev_ref, w_ref, v_ref, *, lc):
    bb_a, bb_b = bba[0], bbb[0]
    cc_a, cc_b = cca[0], ccb[0]
    cmul = lambda k, a, b: pwr[0, k:k + 1, :] * a + pwi[0, k:k + 1, :] * b
    w = jnp.concatenate([cmul(lc - 1 - tau, bb_a, bb_b) for tau in range(lc)], axis=0)
    v = jnp.concatenate([cmul(t + 1, cc_a, cc_b) for t in range(lc)], axis=0)
    w_ref[0] = w.astype(BF16)
    v_ref[0] = v.astype(BF16)
    krev_ref[0] = lax.dot_general(cc_a, w, (((1,), (1,)), ((), ())), precision=lax.Precision.HIGHEST,
                                  preferred_element_type=F32)


def _toeplitz_kernel(krev_ref, m_ref, *, group, lc):
    t = pl.program_id(0)
    g, _, n = krev_ref.shape
    row = lax.broadcasted_iota(jnp.int32, (n, n), 0)
    col = lax.broadcasted_iota(jnp.int32, (n, n), 1)
    shift = (row == col + group * (lc - 1 - t)).astype(BF16)
    krev = krev_ref[...].reshape(g * group, n).astype(BF16)
    res = jnp.dot(krev, shift, preferred_element_type=F32)
    m_ref[...] = res.reshape(g, group, n).astype(BF16)


def _ssm_pack_kernel(u_ref, o_ref, *, lc):
    n_chunks = u_ref.shape[0] // lc
    for tau in range(lc):
        x = u_ref[pl.ds(tau, n_chunks, stride=lc), :]
        o_ref[:, tau * n_chunks:(tau + 1) * n_chunks] = x.T.astype(o_ref.dtype)


def _ssm_unpack_kernel(y_ref, o_ref, *, lc):
    n_chunks = o_ref.shape[0] // lc
    for t in range(lc):
        o_ref[pl.ds(t, n_chunks, stride=lc), :] = y_ref[:, t * n_chunks:(t + 1) * n_chunks].T


def _ssm_main_kernel(ut_ref, m_ref, w_ref, v_ref, ar_ref, ai_ref, yt_ref, *, batch, lc, pack, n_steps):
    group = ut_ref.shape[0]
    seq = ut_ref.shape[1] // batch
    blocks = seq // (pack * lc)
    n_chunks = blocks * pack
    half = w_ref.shape[2] // 2
    chunk = lax.broadcasted_iota(jnp.int32, (n_chunks, 2 * half), 0)
    for b in range(batch):
        lane0 = lambda blk, k: b * seq + blk * (pack * lc) + k * pack
        xt = jnp.concatenate(
            [jnp.concatenate([ut_ref[:, pl.ds(lane0(blk, tau), pack)] for blk in range(blocks)], axis=1)
             for tau in range(lc)], axis=0)
        yt = jnp.dot(m_ref[0], xt, preferred_element_type=F32)
        s = lax.dot_general(xt, w_ref[0], (((0,), (0,)), ((), ())), preferred_element_type=F32)
        for j in range(n_steps):
            k = 1 << j
            sh = jnp.where(chunk >= k, pltpu.roll(s, k, 0), 0.0)
            s = s + ar_ref[0, j:j + 1, :] * sh + ai_ref[0, j:j + 1, :] * pltpu.roll(sh, half, 1)
        s_in = jnp.where(chunk >= 1, pltpu.roll(s, 1, 0), 0.0)
        yt = yt + lax.dot_general(v_ref[0], s_in.astype(BF16), (((1,), (1,)), ((), ())),
                                  preferred_element_type=F32)
        for t in range(lc):
            for blk in range(blocks):
                yt_ref[:, pl.ds(lane0(blk, t), pack)] = yt[t * group:(t + 1) * group, blk * pack:(blk + 1) * pack]


def _ssm_conv(u, batch, log_dt, a_re, a_im, b_re, b_im, c_re, c_im):
    t, d_ssm = u.shape
    g, p = a_re.shape
    h = b_re.shape[-1]
    lc = SSM_CHUNK
    n = lc * h
    seq = t // batch
    pack = V7X_LANES
    pb = pack * lc
    n_steps = max(1, (seq // lc - 1).bit_length())
    assert seq % pb == 0 and d_ssm % V7X_LANES == 0

    lb_re, lb_im, bb_re, bb_im = _ssm_discretise(log_dt, a_re, a_im, b_re, b_im)
    pw_r, pw_i = [jnp.ones_like(lb_re)], [jnp.zeros_like(lb_re)]
    for _ in range(lc):
        pw_r, pw_i = (pw_r + [pw_r[-1] * lb_re - pw_i[-1] * lb_im],
                      pw_i + [pw_r[-1] * lb_im + pw_i[-1] * lb_re])
    pw_r = jnp.stack(pw_r, axis=1)
    pw_i = jnp.stack(pw_i, axis=1)
    pair = lambda a, b: jnp.concatenate([a, b], axis=-1)
    bt_r, bt_i = bb_re.transpose(0, 2, 1), bb_im.transpose(0, 2, 1)
    tables = [pair(pw_r, pw_r), pair(pw_i, pw_i), pair(bt_r, bt_i), pair(-bt_i, bt_r),
              pair(c_re, -c_im), pair(-c_im, -c_re)]
    sr, si = pw_r[:, lc], pw_i[:, lc]
    ar, ai = [], []
    for _ in range(n_steps):
        ar.append(pair(sr, sr))
        ai.append(pair(-si, si))
        sr, si = sr * sr - si * si, 2.0 * sr * si
    ar = jnp.stack(ar, axis=1)
    ai = jnp.stack(ai, axis=1)

    per_group = lambda *shape: pl.BlockSpec((1,) + shape, lambda i: (i,) + (0,) * len(shape))
    krev, w, v = pl.pallas_call(
        functools.partial(_ssm_ops_kernel, lc=lc),
        grid=(g,),
        in_specs=[per_group(lc + 1, 2 * p)] * 2 + [per_group(h, 2 * p)] * 4,
        out_specs=[per_group(h, n), per_group(n, 2 * p), per_group(n, 2 * p)],
        out_shape=[jax.ShapeDtypeStruct((g, h, n), F32),
                   jax.ShapeDtypeStruct((g, n, 2 * p), BF16),
                   jax.ShapeDtypeStruct((g, n, 2 * p), BF16)],
        compiler_params=_params("parallel"),
        name="ssm_ops",
    )(*tables)

    m = pl.pallas_call(
        functools.partial(_toeplitz_kernel, group=h, lc=lc),
        grid=(lc,),
        in_specs=[pl.BlockSpec((g, h, n), lambda i: (0, 0, 0))],
        out_specs=pl.BlockSpec((g, None, h, n), lambda i: (0, i, 0, 0)),
        out_shape=jax.ShapeDtypeStruct((g, lc, h, n), BF16),
        compiler_params=_params("parallel"),
        name="ssm_toeplitz",
    )(krev).reshape(g, n, n)

    tok_blk = pl.BlockSpec((pb, V7X_LANES), lambda i, j: (i, j))
    ch_blk = pl.BlockSpec((V7X_LANES, pb), lambda i, j: (j, i))
    relayout_grid = (t // pb, d_ssm // V7X_LANES)
    ut = pl.pallas_call(
        functools.partial(_ssm_pack_kernel, lc=lc),
        grid=relayout_grid,
        in_specs=[tok_blk],
        out_specs=ch_blk,
        out_shape=jax.ShapeDtypeStruct((d_ssm, t), BF16),
        compiler_params=_params("parallel", "parallel"),
        name="ssm_pack",
    )(u)
    yt = pl.pallas_call(
        functools.partial(_ssm_main_kernel, batch=batch, lc=lc, pack=pack, n_steps=n_steps),
        grid=(g,),
        in_specs=[pl.BlockSpec((h, t), lambda i: (i, 0)), per_group(n, n), per_group(n, 2 * p), per_group(n, 2 * p),
                  per_group(n_steps, 2 * p), per_group(n_steps, 2 * p)],
        out_specs=pl.BlockSpec((h, t), lambda i: (i, 0)),
        out_shape=jax.ShapeDtypeStruct((d_ssm, t), F32),
        compiler_params=_params("parallel"),
        name="ssm_main",
    )(ut, m, w, v, ar, ai)
    return pl.pallas_call(
        functools.partial(_ssm_unpack_kernel, lc=lc),
        grid=relayout_grid,
        in_specs=[ch_blk],
        out_specs=tok_blk,
        out_shape=jax.ShapeDtypeStruct((t, d_ssm), F32),
        compiler_params=_params("parallel", "parallel"),
        name="ssm_unpack",
    )(yt)


def _regroup(n, dil, hop, piece=None):
    rows = n // (dil * hop)
    piece = piece or rows
    for res in range(dil * hop):
        a, r = divmod(res, dil)
        for k in range(0, rows, piece):
            yield (slice(res * rows + k, res * rows + k + piece),
                   pl.ds(r * (n // dil) + a + k * hop, piece, stride=hop))


def _attn_kernel(q_ref, k_ref, v_ref, o_ref, stage, kp4, vp4, kp16, vp16, qf, qp, ob, mb, lb, *, span, blk, unroll):
    s = pl.program_id(2)
    seq, lanes = k_ref.shape[1], k_ref.shape[2]
    dils = [d for _, d in DILATIONS]
    hop = dils[1]
    k_perm = {dils[1]: kp4, dils[2]: kp16}
    v_perm = {dils[1]: vp4, dils[2]: vp16}

    @pl.when(s == 0)
    def _():
        for src, perm in ((k_ref, k_perm), (v_ref, v_perm)):
            stage[0] = src[0].astype(F32)
            for dst, rows in _regroup(seq, dils[0], hop):
                x = stage[0, rows, :]
                stage[1, dst, :] = x
                perm[dils[1]][dst, :] = x.astype(BF16)
            for dst, rows in _regroup(seq, dils[1], hop):
                perm[dils[2]][dst, :] = stage[1, rows, :].astype(BF16)

    scale = 1.4426950408889634 / (HEAD_DIM ** 0.5)
    q0 = q_ref[0].astype(F32) * scale
    qf[0] = q0
    qp[0] = q0.astype(BF16)
    for dst, rows in _regroup(span, dils[0], hop):
        x = qf[0, rows, :]
        qf[1, dst, :] = x
        qp[1, dst, :] = x.astype(BF16)
    for dst, rows in _regroup(span, dils[1], hop):
        qp[2, dst, :] = qf[1, rows, :].astype(BF16)

    head0 = lax.broadcasted_iota(jnp.int32, (blk, lanes), 1) < HEAD_DIM
    head0_kv = lax.broadcasted_iota(jnp.int32, (2 * blk, lanes), 1) < HEAD_DIM
    rel =(lax.broadcasted_iota(jnp.int32, (blk, 2 * blk), 0)
           - lax.broadcasted_iota(jnp.int32, (blk, 2 * blk), 1))

    def run_branches(first_span):
        band = None if first_span else (rel >= -blk) & (rel <= 0)
        for di, dil in enumerate(dils):
            per_res = span // dil
            n_blk = per_res // blk
            res_len = seq // dil

            def body(it, carry, di=di, dil=dil, per_res=per_res, n_blk=n_blk, res_len=res_len):
                res = it // n_blk
                jj = it % n_blk
                qb = s * per_res + jj * blk
                kb = jnp.maximum(qb - blk, 0)
                row0 = pl.multiple_of(res * per_res + jj * blk, blk)
                koff = pl.multiple_of(res * res_len + kb, blk)
                q = qp[di, pl.ds(row0, blk), :]
                if dil == 1:
                    kk = k_ref[0, pl.ds(koff, 2 * blk), :]
                    vv = v_ref[0, pl.ds(koff, 2 * blk), :]
                else:
                    kk = k_perm[dil][pl.ds(koff, 2 * blk), :]
                    vv = v_perm[dil][pl.ds(koff, 2 * blk), :]
                if first_span:
                    lo = kb - qb
                    valid = (rel >= lo) & (rel <= lo + blk)
                else:
                    valid = band
                outs, maxs = [], []
                for head_mask, kv_mask in ((head0, head0_kv), (~head0, ~head0_kv)):
                    qh = jnp.where(head_mask, q, jnp.zeros_like(q))
                    sc = lax.dot_general(qh, kk, (((1,), (1,)), ((), ())), preferred_element_type=F32)
                    sc = jnp.where(valid, sc, NEG_INF)
                    m = jnp.max(sc, axis=-1, keepdims=True)
                    p = jnp.exp2(sc - m)
                    maxs.append(m)
                    outs.append(jnp.dot(p.astype(BF16), jnp.where(kv_mask, vv, jnp.ones_like(vv)),
                                        preferred_element_type=F32))
                ob[di, pl.ds(row0, blk), :] = jnp.where(head0, outs[0], outs[1])
                mb[di, pl.ds(row0, blk), :] = jnp.where(head0, maxs[0], maxs[1])
                lb[di, pl.ds(row0, blk), :] = jnp.where(head0, outs[1], outs[0])
                return carry

            lax.fori_loop(0, dil * n_blk, body, 0, unroll=unroll)

    pl.when(s == 0)(functools.partial(run_branches, True))
    pl.when(s != 0)(functools.partial(run_branches, False))

    unswap = lambda x: pltpu.roll(x, HEAD_DIM, 1)
    piece = 32
    for di in range(1, len(dils)):
        for dst, rows in _regroup(span, dils[di - 1], hop, piece):
            m_a = mb[di - 1, rows, :]
            l_a = lb[di - 1, rows, :]
            if di == 1:
                l_a = unswap(l_a)
            m_b = mb[di, dst, :]
            m_new = jnp.maximum(m_a, m_b)
            w_a = jnp.exp2(m_a - m_new)
            w_b = jnp.exp2(m_b - m_new)
            ob[di, dst, :] = ob[di - 1, rows, :] * w_a + ob[di, dst, :] * w_b
            lb[di, dst, :] = l_a * w_a + unswap(lb[di, dst, :]) * w_b
            mb[di, dst, :] = m_new
    last = len(dils) - 1
    for di in range(last, 0, -1):
        for dst, rows in _regroup(span, dils[di - 1], hop, piece):
            x = ob[di, dst, :]
            ob[di - 1, rows, :] = x / lb[di, dst, :] if di == last else x
    o_ref[0] = ob[0].astype(o_ref.dtype)


def _attention(q, k, v, batch, *, unroll=16):
    t, c = q.shape
    seq = t // batch
    span, blk = ATTN_SPAN, ATTN_BLOCK
    dils = [d for _, d in DILATIONS]
    assert dils == [1, 4, 16] and all(w // d == blk and span % (blk * d) == 0 for w, d in DILATIONS)
    assert seq % span == 0 and V7X_LANES % HEAD_DIM == 0
    q3, k3, v3 = (a.reshape(batch, seq, c) for a in (q, k, v))
    lanes = V7X_LANES
    full = pl.BlockSpec((1, seq, lanes), lambda b, hp, s: (b, 0, hp))
    tile = pl.BlockSpec((1, span, lanes), lambda b, hp, s: (b, s, hp))
    out = pl.pallas_call(
        functools.partial(_attn_kernel, span=span, blk=blk, unroll=unroll),
        grid=(batch, c // lanes, seq // span),
        in_specs=[tile, full, full],
        out_specs=tile,
        out_shape=jax.ShapeDtypeStruct((batch, seq, c), BF16),
        scratch_shapes=[pltpu.VMEM((2, seq, lanes), F32)]
        + [pltpu.VMEM((seq, lanes), BF16)] * 4
        + [pltpu.VMEM((2, span, lanes), F32), pltpu.VMEM((len(dils), span, lanes), BF16)]
        + [pltpu.VMEM((len(dils), span, lanes), F32)] * 3,
        compiler_params=_params("parallel", "parallel", "arbitrary"),
        name="attn",
    )(q3, k3, v3)
    return out.reshape(t, c)


def _post_kernel(yc_ref, u_ref, att_ref, h_ref, d_ref, wglu_ref, bglu_ref, gs_ref, ga_ref, wo_ref, o_ref):
    y = jax.nn.gelu(yc_ref[...] + d_ref[...] * u_ref[...])
    gate = jax.nn.sigmoid(jnp.dot(y.astype(BF16), wglu_ref[...], preferred_element_type=F32) + bglu_ref[...])
    ys = _rmsnorm(y * gate, gs_ref[...]).astype(BF16)
    ya = _rmsnorm(att_ref[...].astype(F32), ga_ref[...]).astype(BF16)
    d_ssm = ys.shape[1]
    o_ref[...] = (h_ref[...]
                  + jnp.dot(ys, wo_ref[:d_ssm, :], preferred_element_type=F32)
                  + jnp.dot(ya, wo_ref[d_ssm:, :], preferred_element_type=F32))


def _post(yc, u, att, h, d, w_glu, b_glu, g_ssm, g_att, w_out, *, tm=512):
    t, dm = h.shape
    c = yc.shape[1]
    row = lambda i: (i, 0)
    const = lambda i: (0, 0)
    return pl.pallas_call(
        _post_kernel,
        grid=(t // tm,),
        in_specs=[pl.BlockSpec((tm, c), row), pl.BlockSpec((tm, c), row), pl.BlockSpec((tm, c), row),
                  pl.BlockSpec((tm, dm), row),
                  pl.BlockSpec((1, c), const), pl.BlockSpec((c, c), const), pl.BlockSpec((1, c), const),
                  pl.BlockSpec((1, c), const), pl.BlockSpec((1, c), const), pl.BlockSpec((2 * c, dm), const)],
        out_specs=pl.BlockSpec((tm, dm), row),
        out_shape=jax.ShapeDtypeStruct((t, dm), F32),
        compiler_params=_params("parallel"),
        name="post",
    )(yc, u, att, h, d.reshape(1, c), w_glu, b_glu.reshape(1, c), g_ssm.reshape(1, c), g_att.reshape(1, c), w_out)


def kernel(x, ffn1_norm, ffn1_w_gate, ffn1_w_up, ffn1_w_down, mix_norm, w_in, ssm_log_dt, ssm_a_re, ssm_a_im, ssm_b_re, ssm_b_im, ssm_c_re, ssm_c_im, ssm_d, ssm_w_glu, ssm_b_glu, ssm_out_norm, attn_out_norm, w_out, ffn2_norm, ffn2_w_gate, ffn2_w_up, ffn2_w_down, final_norm):
    batch, seq, d_model = x.shape
    depth = ffn1_norm.shape[0]
    h = x.reshape(batch * seq, d_model)
    for l in range(depth):
        last = l == depth - 1
        ffn_w = lambda *ws: [w.astype(BF16) for w in ws]
        h = _ffn(h, ffn1_norm[l], *ffn_w(ffn1_w_gate[l], ffn1_w_up[l], ffn1_w_down[l]))
        q, k, v, u = _inproj(h, mix_norm[l], w_in[l].astype(BF16))
        yc = _ssm_conv(u, batch, ssm_log_dt[l], ssm_a_re[l], ssm_a_im[l], ssm_b_re[l], ssm_b_im[l],
                       ssm_c_re[l], ssm_c_im[l])
        att = _attention(q, k, v, batch)
        h = _post(yc, u, att, h, ssm_d[l], ssm_w_glu[l].astype(BF16), ssm_b_glu[l], ssm_out_norm[l],
                  attn_out_norm[l], w_out[l].astype(BF16))
        h = _ffn(h, ffn2_norm[l], *ffn_w(ffn2_w_gate[l], ffn2_w_up[l], ffn2_w_down[l]),
                 final_g=final_norm if last else None)
    return h.reshape(batch, seq, d_model)
```

```python
import functools

import jax
import jax.numpy as jnp
from jax import lax
from jax.experimental import pallas as pl
from jax.experimental.pallas import tpu as pltpu

F32 = jnp.float32
BF16 = jnp.bfloat16

EPS = 1e-6
NEG_INF = -1e30
HEAD_DIM = 64
SSM_GROUP = 16
DILATIONS = ((128, 1), (512, 4), (2048, 16))

V7X_LANES = 128
V7X_VMEM_BYTES = 64 * 1024 * 1024
VMEM_LIMIT_BYTES = V7X_VMEM_BYTES - 6 * 1024 * 1024

SSM_CHUNK = 32
ATTN_SPAN = 2048
ATTN_BLOCK = 128


def _rmsnorm(x, g):
    return x * lax.rsqrt(jnp.mean(x * x, axis=-1, keepdims=True) + EPS) * g


def _params(*semantics):
    return pltpu.CompilerParams(dimension_semantics=semantics, vmem_limit_bytes=VMEM_LIMIT_BYTES)


def _ffn_kernel(*refs, n_f, n_chunks, overlap, final_norm):
    if final_norm:
        h_ref, g_ref, wg_ref, wu_ref, wd_ref, fg_ref, o_ref, n_ref = refs
    else:
        h_ref, g_ref, wg_ref, wu_ref, wd_ref, o_ref, n_ref = refs
    j = pl.program_id(1)

    @pl.when(j == 0)
    def _():
        h = h_ref[...]
        n_ref[...] = _rmsnorm(h, g_ref[...]).astype(BF16)
        o_ref[...] = h

    n = n_ref[...]
    gate = jnp.dot(n, wg_ref[...], preferred_element_type=F32)
    up = jnp.dot(n, wu_ref[...], preferred_element_type=F32)
    a = 0.5 * (gate * jax.nn.sigmoid(gate)) * up
    if overlap:
        col = lax.broadcasted_iota(jnp.int32, a.shape, 1)
        a = jnp.where(col >= jnp.where(j == n_f - 1, overlap, 0), a, 0.0)
    a = a.astype(BF16)
    cw = o_ref.shape[1] // n_chunks
    for c in range(n_chunks):
        sl = slice(c * cw, (c + 1) * cw)
        o_ref[:, sl] += jnp.dot(a, wd_ref[:, sl], preferred_element_type=F32)

    if final_norm:
        @pl.when(j == n_f - 1)
        def _():
            o_ref[...] = _rmsnorm(o_ref[...], fg_ref[...])


def _ffn(h, norm_g, wg, wu, wd, final_g=None, *, tm=1024, tf=512):
    t, d = h.shape
    f = wg.shape[1]
    n_f = pl.cdiv(f, tf)
    overlap = n_f * tf - f
    assert f >= tf and overlap % V7X_LANES == 0
    final_norm = final_g is not None
    row = lambda i, j: (i, 0)
    start = lambda j: pl.multiple_of(jnp.minimum(j * tf, f - tf), V7X_LANES)
    in_specs = [
        pl.BlockSpec((tm, d), row),
        pl.BlockSpec((1, d), lambda i, j: (0, 0)),
        pl.BlockSpec((pl.Element(d), pl.Element(tf)), lambda i, j: (0, start(j))),
        pl.BlockSpec((pl.Element(d), pl.Element(tf)), lambda i, j: (0, start(j))),
        pl.BlockSpec((pl.Element(tf), pl.Element(d)), lambda i, j: (start(j), 0)),
    ]
    args = [h, norm_g.reshape(1, d), wg, wu, wd]
    if final_norm:
        in_specs.append(pl.BlockSpec((1, d), lambda i, j: (0, 0)))
        args.append(final_g.reshape(1, d))
    return pl.pallas_call(
        functools.partial(_ffn_kernel, n_f=n_f, n_chunks=4, overlap=overlap, final_norm=final_norm),
        grid=(t // tm, n_f),
        in_specs=in_specs,
        out_specs=pl.BlockSpec((tm, d), row),
        out_shape=jax.ShapeDtypeStruct((t, d), F32),
        scratch_shapes=[pltpu.VMEM((tm, d), BF16)],
        compiler_params=_params("parallel", "arbitrary"),
        name="ffn_final" if final_norm else "ffn",
    )(*args)


def _inproj_kernel(h_ref, g_ref, w_ref, qkv_ref, u_ref, n_ref, *, n_qkv):
    j = pl.program_id(1)

    @pl.when(j == 0)
    def _():
        n_ref[...] = _rmsnorm(h_ref[...], g_ref[...]).astype(BF16)

    @pl.when(j < n_qkv)
    def _():
        qkv_ref[...] = jnp.dot(n_ref[...], w_ref[...], preferred_element_type=F32).astype(qkv_ref.dtype)

    @pl.when(j == n_qkv)
    def _():
        u_ref[...] = jnp.dot(n_ref[...], w_ref[...], preferred_element_type=F32)


def _inproj(h, norm_g, w_in, *, tm=1024):
    t, d = h.shape
    n_qkv = 3
    c = w_in.shape[1] // (n_qkv + 1)
    return pl.pallas_call(
        functools.partial(_inproj_kernel, n_qkv=n_qkv),
        grid=(t // tm, n_qkv + 1),
        in_specs=[
            pl.BlockSpec((tm, d), lambda i, j: (i, 0)),
            pl.BlockSpec((1, d), lambda i, j: (0, 0)),
            pl.BlockSpec((d, c), lambda i, j: (0, j)),
        ],
        out_specs=[pl.BlockSpec((tm, c), lambda i, j: (i, jnp.minimum(j, n_qkv - 1))),
                   pl.BlockSpec((tm, c), lambda i, j: (i, 0))],
        out_shape=[jax.ShapeDtypeStruct((t, n_qkv * c), BF16), jax.ShapeDtypeStruct((t, c), F32)],
        scratch_shapes=[pltpu.VMEM((tm, d), BF16)],
        compiler_params=_params("parallel", "arbitrary"),
        name="inproj",
    )(h, norm_g.reshape(1, d), w_in)


def _ssm_discretise(log_dt, a_re, a_im, b_re, b_im):
    dt = jnp.exp(log_dt)[:, None]
    mag = jnp.exp(a_re * dt)
    lb_re = mag * jnp.cos(a_im * dt)
    lb_im = mag * jnp.sin(a_im * dt)
    den = a_re * a_re + a_im * a_im
    nr = lb_re - 1.0
    ni = lb_im
    f_re = (nr * a_re + ni * a_im) / den
    f_im = (ni * a_re - nr * a_im) / den
    bb_re = f_re[..., None] * b_re - f_im[..., None] * b_im
    bb_im = f_re[..., None] * b_im + f_im[..., None] * b_re
    return lb_re, lb_im, bb_re, bb_im


def _ssm_ops_kernel(pwr, pwi, bba, bbb, cca, crr, cii, krev_ref, w_ref, vre_ref, vim_ref, *, lc):
    cmul = lambda k, a, b: pwr[0, k:k + 1, :] * a + pwi[0, k:k + 1, :] * b
    w = jnp.concatenate([cmul(lc - 1 - tau, bba[0], bbb[0]) for tau in range(lc)], axis=0)
    w_ref[0] = w.astype(BF16)
    vre_ref[0] = jnp.concatenate([cmul(t + 1, crr[0], -cii[0]) for t in range(lc)], axis=0).astype(BF16)
    vim_ref[0] = jnp.concatenate([cmul(t + 1, -cii[0], -crr[0]) for t in range(lc)], axis=0).astype(BF16)
    krev_ref[0] = lax.dot_general(cca[0], w, (((1,), (1,)), ((), ())), precision=lax.Precision.HIGHEST,
                                  preferred_element_type=F32)


def _toeplitz_kernel(krev_ref, m_ref, *, group, lc):
    t = pl.program_id(0)
    g, _, n = krev_ref.shape
    row = lax.broadcasted_iota(jnp.int32, (n, n), 0)
    col = lax.broadcasted_iota(jnp.int32, (n, n), 1)
    shift = (row == col + group * (lc - 1 - t)).astype(BF16)
    krev = krev_ref[...].reshape(g * group, n).astype(BF16)
    res = jnp.dot(krev, shift, preferred_element_type=F32)
    m_ref[...] = res.reshape(g, group, n).astype(BF16)


SSM_HOP = 4


def _lag_rows(tau, n, lc):
    return pl.ds((tau % SSM_HOP) * (n // SSM_HOP) + tau // SSM_HOP, n // lc, stride=lc // SSM_HOP)


def _ssm_pack_kernel(u_ref, o_ref, hop_ref, *, lc):
    n = u_ref.shape[0]
    n_chunks = n // lc
    for dst, rows in _regroup(n, 1, SSM_HOP):
        hop_ref[dst, :] = u_ref[rows, :]
    for tau in range(lc):
        x = hop_ref[_lag_rows(tau, n, lc), :]
        o_ref[:, tau * n_chunks:(tau + 1) * n_chunks] = x.T.astype(o_ref.dtype)


def _ssm_unpack_kernel(y_ref, o_ref, hop_ref, *, lc):
    n = o_ref.shape[0]
    n_chunks = n // lc
    for t in range(lc):
        hop_ref[_lag_rows(t, n, lc), :] = y_ref[:, t * n_chunks:(t + 1) * n_chunks].T
    for src, rows in _regroup(n, 1, SSM_HOP):
        o_ref[rows, :] = hop_ref[src, :]


def _ssm_main_kernel(ut_ref, m_ref, w_ref, vre_ref, vim_ref, ar_ref, ai_ref, yt_ref, *, batch, lc, pack, n_steps):
    group = ut_ref.shape[0]
    seq = ut_ref.shape[1] // batch
    blocks = seq // (pack * lc)
    n_chunks = blocks * pack
    lanes = w_ref.shape[2]
    half = lanes // 2
    nt = (((1,), (1,)), ((), ()))
    chunk = lax.broadcasted_iota(jnp.int32, (n_chunks, lanes), 0)
    first = lax.broadcasted_iota(jnp.int32, (n_chunks, lanes), 1) < half
    first_v = lax.broadcasted_iota(jnp.int32, vre_ref.shape[1:], 1) < half
    lane0 = lambda b, blk, k: b * seq + blk * (pack * lc) + k * pack

    def load_xt(b):
        return jnp.concatenate(
            [jnp.concatenate([ut_ref[:, pl.ds(lane0(b, blk, tau), pack)] for blk in range(blocks)], axis=1)
             for tau in range(lc)], axis=0)

    for b0 in range(0, batch, 2):
        xts = [load_xt(b0), load_xt(b0 + 1)]
        sa, sb = [lax.dot_general(xt, w_ref[0], (((0,), (0,)), ((), ())), preferred_element_type=F32) for xt in xts]
        s_re = jnp.where(first, sa, pltpu.roll(sb, half, 1))
        s_im = jnp.where(first, pltpu.roll(sa, half, 1), sb)
        for j in range(n_steps):
            k = 1 << j
            shift = lambda x: jnp.where(chunk >= k, pltpu.roll(x, k, 0), 0.0)
            sh_re, sh_im = shift(s_re), shift(s_im)
            a_re, a_im = ar_ref[0, j:j + 1, :], ai_ref[0, j:j + 1, :]
            s_re, s_im = s_re + a_re * sh_re - a_im * sh_im, s_im + a_re * sh_im + a_im * sh_re
        enter = lambda x: jnp.where(chunk >= 1, pltpu.roll(x, 1, 0), 0.0).astype(BF16)
        in_re, in_im = enter(s_re), enter(s_im)
        for i, xt in enumerate(xts):
            own = first_v if i == 0 else ~first_v
            keep = lambda v: jnp.where(own, v, jnp.zeros_like(v))
            yt = (jnp.dot(m_ref[0], xt, preferred_element_type=F32)
                  + lax.dot_general(keep(vre_ref[0]), in_re, nt, preferred_element_type=F32)
                  + lax.dot_general(keep(vim_ref[0]), in_im, nt, preferred_element_type=F32))
            for t in range(lc):
                for blk in range(blocks):
                    yt_ref[:, pl.ds(lane0(b0 + i, blk, t), pack)] = (
                        yt[t * group:(t + 1) * group, blk * pack:(blk + 1) * pack])


def _ssm_conv(u, batch, log_dt, a_re, a_im, b_re, b_im, c_re, c_im):
    t, d_ssm = u.shape
    g, p = a_re.shape
    h = b_re.shape[-1]
    lc = SSM_CHUNK
    n = lc * h
    seq = t // batch
    pack = V7X_LANES
    pb = pack * lc
    n_steps = max(1, (seq // lc - 1).bit_length())
    assert seq % pb == 0 and d_ssm % V7X_LANES == 0 and batch % 2 == 0

    lb_re, lb_im, bb_re, bb_im = _ssm_discretise(log_dt, a_re, a_im, b_re, b_im)
    pw_r, pw_i = [jnp.ones_like(lb_re)], [jnp.zeros_like(lb_re)]
    for _ in range(lc):
        pw_r, pw_i = (pw_r + [pw_r[-1] * lb_re - pw_i[-1] * lb_im],
                      pw_i + [pw_r[-1] * lb_im + pw_i[-1] * lb_re])
    pw_r = jnp.stack(pw_r, axis=1)
    pw_i = jnp.stack(pw_i, axis=1)
    pair = lambda a, b: jnp.concatenate([a, b], axis=-1)
    bt_r, bt_i = bb_re.transpose(0, 2, 1), bb_im.transpose(0, 2, 1)
    tables = [pair(pw_r, pw_r), pair(pw_i, pw_i), pair(bt_r, bt_i), pair(-bt_i, bt_r),
              pair(c_re, -c_im), pair(c_re, c_re), pair(c_im, c_im)]
    sr, si = pw_r[:, lc], pw_i[:, lc]
    ar, ai = [], []
    for _ in range(n_steps):
        ar.append(pair(sr, sr))
        ai.append(pair(si, si))
        sr, si = sr * sr - si * si, 2.0 * sr * si
    ar = jnp.stack(ar, axis=1)
    ai = jnp.stack(ai, axis=1)

    per_group = lambda *shape: pl.BlockSpec((1,) + shape, lambda i: (i,) + (0,) * len(shape))
    krev, w, vre, vim = pl.pallas_call(
        functools.partial(_ssm_ops_kernel, lc=lc),
        grid=(g,),
        in_specs=[per_group(lc + 1, 2 * p)] * 2 + [per_group(h, 2 * p)] * 5,
        out_specs=[per_group(h, n)] + [per_group(n, 2 * p)] * 3,
        out_shape=[jax.ShapeDtypeStruct((g, h, n), F32)] + [jax.ShapeDtypeStruct((g, n, 2 * p), BF16)] * 3,
        compiler_params=_params("parallel"),
        name="ssm_ops",
    )(*tables)

    m = pl.pallas_call(
        functools.partial(_toeplitz_kernel, group=h, lc=lc),
        grid=(lc,),
        in_specs=[pl.BlockSpec((g, h, n), lambda i: (0, 0, 0))],
        out_specs=pl.BlockSpec((g, None, h, n), lambda i: (0, i, 0, 0)),
        out_shape=jax.ShapeDtypeStruct((g, lc, h, n), BF16),
        compiler_params=_params("parallel"),
        name="ssm_toeplitz",
    )(krev).reshape(g, n, n)

    tok_blk = pl.BlockSpec((pb, V7X_LANES), lambda i, j: (i, j))
    ch_blk = pl.BlockSpec((V7X_LANES, pb), lambda i, j: (j, i))
    relayout_grid = (t // pb, d_ssm // V7X_LANES)
    ut = pl.pallas_call(
        functools.partial(_ssm_pack_kernel, lc=lc),
        grid=relayout_grid,
        in_specs=[tok_blk],
        out_specs=ch_blk,
        out_shape=jax.ShapeDtypeStruct((d_ssm, t), BF16),
        scratch_shapes=[pltpu.VMEM((pb, V7X_LANES), F32)],
        compiler_params=_params("parallel", "parallel"),
        name="ssm_pack",
    )(u)
    yt = pl.pallas_call(
        functools.partial(_ssm_main_kernel, batch=batch, lc=lc, pack=pack, n_steps=n_steps),
        grid=(g,),
        in_specs=[pl.BlockSpec((h, t), lambda i: (i, 0)), per_group(n, n)] + [per_group(n, 2 * p)] * 3
        + [per_group(n_steps, 2 * p)] * 2,
        out_specs=pl.BlockSpec((h, t), lambda i: (i, 0)),
        out_shape=jax.ShapeDtypeStruct((d_ssm, t), F32),
        compiler_params=_params("parallel"),
        name="ssm_main",
    )(ut, m, w, vre, vim, ar, ai)
    return pl.pallas_call(
        functools.partial(_ssm_unpack_kernel, lc=lc),
        grid=relayout_grid,
        in_specs=[ch_blk],
        out_specs=tok_blk,
        out_shape=jax.ShapeDtypeStruct((t, d_ssm), F32),
        scratch_shapes=[pltpu.VMEM((pb, V7X_LANES), F32)],
        compiler_params=_params("parallel", "parallel"),
        name="ssm_unpack",
    )(yt)


def _regroup(n, dil, hop, piece=None):
    rows = n // (dil * hop)
    piece = piece or rows
    for res in range(dil * hop):
        a, r = divmod(res, dil)
        for k in range(0, rows, piece):
            yield (slice(res * rows + k, res * rows + k + piece),
                   pl.ds(r * (n // dil) + a + k * hop, piece, stride=hop))


def _attn_kernel(q_ref, k_ref, v_ref, o_ref, stage, kp4, vp4, kp16, vp16, qf, qp, ob, mb, lb, *, span, blk, unroll):
    s = pl.program_id(2)
    seq, lanes = k_ref.shape[1], k_ref.shape[2]
    dils = [d for _, d in DILATIONS]
    hop = dils[1]
    k_perm = {dils[1]: kp4, dils[2]: kp16}
    v_perm = {dils[1]: vp4, dils[2]: vp16}

    @pl.when(s == 0)
    def _():
        for src, perm in ((k_ref, k_perm), (v_ref, v_perm)):
            stage[0] = src[0].astype(F32)
            for dst, rows in _regroup(seq, dils[0], hop):
                x = stage[0, rows, :]
                stage[1, dst, :] = x
                perm[dils[1]][dst, :] = x.astype(BF16)
            for dst, rows in _regroup(seq, dils[1], hop):
                perm[dils[2]][dst, :] = stage[1, rows, :].astype(BF16)

    scale = 1.4426950408889634 / (HEAD_DIM ** 0.5)
    q0 = q_ref[0].astype(F32) * scale
    qf[0] = q0
    qp[0] = q0.astype(BF16)
    for dst, rows in _regroup(span, dils[0], hop):
        x = qf[0, rows, :]
        qf[1, dst, :] = x
        qp[1, dst, :] = x.astype(BF16)
    for dst, rows in _regroup(span, dils[1], hop):
        qp[2, dst, :] = qf[1, rows, :].astype(BF16)

    head0 = lax.broadcasted_iota(jnp.int32, (blk, lanes), 1) < HEAD_DIM
    head0_kv = lax.broadcasted_iota(jnp.int32, (2 * blk, lanes), 1) < HEAD_DIM
    rel =(lax.broadcasted_iota(jnp.int32, (blk, 2 * blk), 0)
           - lax.broadcasted_iota(jnp.int32, (blk, 2 * blk), 1))

    def run_branches(first_span):
        band = None if first_span else (rel >= -blk) & (rel <= 0)
        for di, dil in enumerate(dils):
            per_res = span // dil
            n_blk = per_res // blk
            res_len = seq // dil

            def body(it, carry, di=di, dil=dil, per_res=per_res, n_blk=n_blk, res_len=res_len):
                res = it // n_blk
                jj = it % n_blk
                qb = s * per_res + jj * blk
                kb = jnp.maximum(qb - blk, 0)
                row0 = pl.multiple_of(res * per_res + jj * blk, blk)
                koff = pl.multiple_of(res * res_len + kb, blk)
                q = qp[di, pl.ds(row0, blk), :]
                if dil == 1:
                    kk = k_ref[0, pl.ds(koff, 2 * blk), :]
                    vv = v_ref[0, pl.ds(koff, 2 * blk), :]
                else:
                    kk = k_perm[dil][pl.ds(koff, 2 * blk), :]
                    vv = v_perm[dil][pl.ds(koff, 2 * blk), :]
                if first_span:
                    lo = kb - qb
                    valid = (rel >= lo) & (rel <= lo + blk)
                else:
                    valid = band
                outs, maxs = [], []
                for head_mask, kv_mask in ((head0, head0_kv), (~head0, ~head0_kv)):
                    qh = jnp.where(head_mask, q, jnp.zeros_like(q))
                    sc = lax.dot_general(qh, kk, (((1,), (1,)), ((), ())), preferred_element_type=F32)
                    sc = jnp.where(valid, sc, NEG_INF)
                    m = jnp.max(sc, axis=-1, keepdims=True)
                    p = jnp.exp2(sc - m)
                    maxs.append(m)
                    outs.append(jnp.dot(p.astype(BF16), jnp.where(kv_mask, vv, jnp.ones_like(vv)),
                                        preferred_element_type=F32))
                ob[di, pl.ds(row0, blk), :] = jnp.where(head0, outs[0], outs[1])
                mb[di, pl.ds(row0, blk), :] = jnp.where(head0, maxs[0], maxs[1])
                lb[di, pl.ds(row0, blk), :] = jnp.where(head0, outs[1], outs[0])
                return carry

            lax.fori_loop(0, dil * n_blk, body, 0, unroll=unroll)

    pl.when(s == 0)(functools.partial(run_branches, True))
    pl.when(s != 0)(functools.partial(run_branches, False))

    unswap = lambda x: pltpu.roll(x, HEAD_DIM, 1)
    piece = 32
    last = len(dils) - 1
    for di in range(last, 0, -1):
        for src, rows in _regroup(span, dils[di - 1], hop, piece):
            m_a = mb[di - 1, rows, :]
            m_b = mb[di, src, :]
            l_b = lb[di, src, :]
            if di == last:
                l_b = unswap(l_b)
            m_new = jnp.maximum(m_a, m_b)
            w_a = jnp.exp2(m_a - m_new)
            w_b = jnp.exp2(m_b - m_new)
            ob[di - 1, rows, :] = ob[di - 1, rows, :] * w_a + ob[di, src, :] * w_b
            lb[di - 1, rows, :] = unswap(lb[di - 1, rows, :]) * w_a + l_b * w_b
            mb[di - 1, rows, :] = m_new
    o_ref[0] = (ob[0] / lb[0]).astype(o_ref.dtype)


def _attention(qkv, batch, *, unroll=16):
    t, c = qkv.shape[0], qkv.shape[1] // 3
    seq = t // batch
    span, blk = ATTN_SPAN, ATTN_BLOCK
    dils = [d for _, d in DILATIONS]
    assert dils == [1, 4, 16] and all(w // d == blk and span % (blk * d) == 0 for w, d in DILATIONS)
    assert seq % span == 0 and V7X_LANES % HEAD_DIM == 0
    qkv3 = qkv.reshape(batch, seq, 3 * c)
    lanes = V7X_LANES
    n_hp = c // lanes
    full = lambda part: pl.BlockSpec((1, seq, lanes), lambda b, hp, s: (b, 0, part * n_hp + hp))
    tile = pl.BlockSpec((1, span, lanes), lambda b, hp, s: (b, s, hp))
    out = pl.pallas_call(
        functools.partial(_attn_kernel, span=span, blk=blk, unroll=unroll),
        grid=(batch, n_hp, seq // span),
        in_specs=[tile, full(1), full(2)],
        out_specs=tile,
        out_shape=jax.ShapeDtypeStruct((batch, seq, c), BF16),
        scratch_shapes=[pltpu.VMEM((2, seq, lanes), F32)]
        + [pltpu.VMEM((seq, lanes), BF16)] * 4
        + [pltpu.VMEM((2, span, lanes), F32), pltpu.VMEM((len(dils), span, lanes), BF16)]
        + [pltpu.VMEM((len(dils), span, lanes), F32)] * 3,
        compiler_params=_params("parallel", "parallel", "arbitrary"),
        name="attn",
    )(qkv3, qkv3, qkv3)
    return out.reshape(t, c)


def _post_kernel(yc_ref, u_ref, att_ref, h_ref, d_ref, wglu_ref, bglu_ref, gs_ref, ga_ref, wo_ref, o_ref):
    y = jax.nn.gelu(yc_ref[...] + d_ref[...] * u_ref[...])
    gate = jax.nn.sigmoid(jnp.dot(y.astype(BF16), wglu_ref[...], preferred_element_type=F32) + bglu_ref[...])
    ys = _rmsnorm(y * gate, gs_ref[...]).astype(BF16)
    ya = _rmsnorm(att_ref[...].astype(F32), ga_ref[...]).astype(BF16)
    d_ssm = ys.shape[1]
    o_ref[...] = (h_ref[...]
                  + jnp.dot(ys, wo_ref[:d_ssm, :], preferred_element_type=F32)
                  + jnp.dot(ya, wo_ref[d_ssm:, :], preferred_element_type=F32))


def _post(yc, u, att, h, d, w_glu, b_glu, g_ssm, g_att, w_out, *, tm=512):
    t, dm = h.shape
    c = yc.shape[1]
    row = lambda i: (i, 0)
    const = lambda i: (0, 0)
    return pl.pallas_call(
        _post_kernel,
        grid=(t // tm,),
        in_specs=[pl.BlockSpec((tm, c), row), pl.BlockSpec((tm, c), row), pl.BlockSpec((tm, c), row),
                  pl.BlockSpec((tm, dm), row),
                  pl.BlockSpec((1, c), const), pl.BlockSpec((c, c), const), pl.BlockSpec((1, c), const),
                  pl.BlockSpec((1, c), const), pl.BlockSpec((1, c), const), pl.BlockSpec((2 * c, dm), const)],
        out_specs=pl.BlockSpec((tm, dm), row),
        out_shape=jax.ShapeDtypeStruct((t, dm), F32),
        compiler_params=_params("parallel"),
        name="post",
    )(yc, u, att, h, d.reshape(1, c), w_glu, b_glu.reshape(1, c), g_ssm.reshape(1, c), g_att.reshape(1, c), w_out)


def kernel(x, ffn1_norm, ffn1_w_gate, ffn1_w_up, ffn1_w_down, mix_norm, w_in, ssm_log_dt, ssm_a_re, ssm_a_im, ssm_b_re, ssm_b_im, ssm_c_re, ssm_c_im, ssm_d, ssm_w_glu, ssm_b_glu, ssm_out_norm, attn_out_norm, w_out, ffn2_norm, ffn2_w_gate, ffn2_w_up, ffn2_w_down, final_norm):
    batch, seq, d_model = x.shape
    depth = ffn1_norm.shape[0]
    h = x.reshape(batch * seq, d_model)
    for l in range(depth):
        last = l == depth - 1
        ffn_w = lambda *ws: [w.astype(BF16) for w in ws]
        h = _ffn(h, ffn1_norm[l], *ffn_w(ffn1_w_gate[l], ffn1_w_up[l], ffn1_w_down[l]))
        qkv, u = _inproj(h, mix_norm[l], w_in[l].astype(BF16))
        yc = _ssm_conv(u, batch, ssm_log_dt[l], ssm_a_re[l], ssm_a_im[l], ssm_b_re[l], ssm_b_im[l],
                       ssm_c_re[l], ssm_c_im[l])
        att = _attention(qkv, batch)
        h = _post(yc, u, att, h, ssm_d[l], ssm_w_glu[l].astype(BF16), ssm_b_glu[l], ssm_out_norm[l],
                  attn_out_norm[l], w_out[l].astype(BF16))
        h = _ffn(h, ffn2_norm[l], *ffn_w(ffn2_w_gate[l], ffn2_w_up[l], ffn2_w_down[l]),
                 final_g=final_norm if last else None)
    return h.reshape(batch, seq, d_model)
```

```python
import functools

import jax
import jax.numpy as jnp
from jax import lax
from jax.experimental import pallas as pl
from jax.experimental.pallas import tpu as pltpu

F32 = jnp.float32
BF16 = jnp.bfloat16

EPS = 1e-6
NEG_INF = -1e30
HEAD_DIM = 64
SSM_GROUP = 16
DILATIONS = ((128, 1), (512, 4), (2048, 16))

V7X_LANES = 128
V7X_VMEM_BYTES = 64 * 1024 * 1024
VMEM_LIMIT_BYTES = V7X_VMEM_BYTES - 6 * 1024 * 1024

SSM_CHUNK = 32
ATTN_SPAN = 2048
ATTN_BLOCK = 128


def _rmsnorm(x, g):
    return x * lax.rsqrt(jnp.mean(x * x, axis=-1, keepdims=True) + EPS) * g


def _params(*semantics):
    return pltpu.CompilerParams(dimension_semantics=semantics, vmem_limit_bytes=VMEM_LIMIT_BYTES)


FFN_BLOCKS_PER_STEP = 2


def _ffn_kernel(*refs, n_f, n_chunks, overlap, final_norm):
    per_step = FFN_BLOCKS_PER_STEP
    h_hbm, g_ref = refs[:2]
    w_refs = [refs[2 + 3 * k:5 + 3 * k] for k in range(per_step)]
    rest = refs[2 + 3 * per_step:]
    fg_ref = rest[0] if final_norm else None
    o_hbm, acc, n_ref, in_sem, out_sem = rest[1:] if final_norm else rest
    i, j = pl.program_id(0), pl.program_id(1)
    n_blocks, n_steps = pl.num_programs(0), pl.num_programs(1)
    tm = acc.shape[1]
    slot = i % 2

    def fetch(blk, s):
        return pltpu.make_async_copy(h_hbm.at[pl.ds(blk * tm, tm), :], acc.at[s], in_sem.at[s])

    def write_back(blk, s):
        return pltpu.make_async_copy(acc.at[s], o_hbm.at[pl.ds(blk * tm, tm), :], out_sem.at[s])

    @pl.when(j == 0)
    def _():
        @pl.when(i == 0)
        def _():
            fetch(0, 0).start()
        fetch(i, slot).wait()
        n_ref[...] = _rmsnorm(acc[slot], g_ref[...]).astype(BF16)

    @pl.when(j == 1)
    def _():
        @pl.when(i >= 1)
        def _():
            write_back(i - 1, 1 - slot).wait()

        @pl.when(i + 1 < n_blocks)
        def _():
            fetch(i + 1, 1 - slot).start()

    def hidden_block(wg_ref, wu_ref, wd_ref, masked):
        n = n_ref[...]
        gate = jnp.dot(n, wg_ref[...], preferred_element_type=F32)
        up = jnp.dot(n, wu_ref[...], preferred_element_type=F32)
        a = 0.5 * (gate * jax.nn.sigmoid(gate)) * up
        if masked:
            a = jnp.where(lax.broadcasted_iota(jnp.int32, a.shape, 1) >= overlap, a, 0.0)
        a = a.astype(BF16)
        cw = acc.shape[2] // n_chunks
        for c in range(n_chunks):
            sl = slice(c * cw, (c + 1) * cw)
            acc[slot, :, sl] += jnp.dot(a, wd_ref[:, sl], preferred_element_type=F32)

    tail = n_f - per_step * (n_f // per_step) or per_step

    @pl.when(j < n_steps - 1)
    def _():
        for k in range(per_step):
            hidden_block(*w_refs[k], masked=False)

    @pl.when(j == n_steps - 1)
    def _():
        for k in range(tail):
            hidden_block(*w_refs[k], masked=bool(overlap) and k == tail - 1)
        if final_norm:
            acc[slot] = _rmsnorm(acc[slot], fg_ref[...])
        write_back(i, slot).start()

        @pl.when(i == n_blocks - 1)
        def _():
            write_back(i, slot).wait()


def _ffn(h, norm_g, wg, wu, wd, final_g=None, *, tm=1024, tf=512):
    t, d = h.shape
    f = wg.shape[1]
    per_step = FFN_BLOCKS_PER_STEP
    n_f = pl.cdiv(f, tf)
    n_steps = pl.cdiv(n_f, per_step)
    overlap = n_f * tf - f
    assert f >= tf and overlap % V7X_LANES == 0 and n_steps >= 2 and t % tm == 0
    final_norm = final_g is not None

    def start(k):
        last_k = n_f - 1 - (n_f - 1 - k) % per_step
        return lambda j: pl.multiple_of(jnp.minimum(jnp.minimum(per_step * j + k, last_k) * tf, f - tf), V7X_LANES)

    in_specs = [pl.BlockSpec(memory_space=pl.ANY), pl.BlockSpec((1, d), lambda i, j: (0, 0))]
    args = [h, norm_g.reshape(1, d)]
    for k in range(per_step):
        in_specs += [
            pl.BlockSpec((pl.Element(d), pl.Element(tf)), lambda i, j, s=start(k): (0, s(j))),
            pl.BlockSpec((pl.Element(d), pl.Element(tf)), lambda i, j, s=start(k): (0, s(j))),
            pl.BlockSpec((pl.Element(tf), pl.Element(d)), lambda i, j, s=start(k): (s(j), 0)),
        ]
        args += [wg, wu, wd]
    if final_norm:
        in_specs.append(pl.BlockSpec((1, d), lambda i, j: (0, 0)))
        args.append(final_g.reshape(1, d))
    return pl.pallas_call(
        functools.partial(_ffn_kernel, n_f=n_f, n_chunks=4, overlap=overlap, final_norm=final_norm),
        grid=(t // tm, n_steps),
        in_specs=in_specs,
        out_specs=pl.BlockSpec(memory_space=pl.ANY),
        out_shape=jax.ShapeDtypeStruct((t, d), F32),
        scratch_shapes=[pltpu.VMEM((2, tm, d), F32), pltpu.VMEM((tm, d), BF16),
                        pltpu.SemaphoreType.DMA((2,)), pltpu.SemaphoreType.DMA((2,))],
        compiler_params=_params("arbitrary", "arbitrary"),
        name="ffn_final" if final_norm else "ffn",
    )(*args)


def _inproj_kernel(h_ref, g_ref, w_ref, qkv_ref, u_ref, n_ref, *, n_qkv):
    j = pl.program_id(1)

    @pl.when(j == 0)
    def _():
        n_ref[...] = _rmsnorm(h_ref[...], g_ref[...]).astype(BF16)

    @pl.when(j < n_qkv)
    def _():
        qkv_ref[...] = jnp.dot(n_ref[...], w_ref[...], preferred_element_type=F32).astype(qkv_ref.dtype)

    @pl.when(j == n_qkv)
    def _():
        u_ref[...] = jnp.dot(n_ref[...], w_ref[...], preferred_element_type=F32)


def _inproj(h, norm_g, w_in, *, tm=1024):
    t, d = h.shape
    n_qkv = 3
    c = w_in.shape[1] // (n_qkv + 1)
    return pl.pallas_call(
        functools.partial(_inproj_kernel, n_qkv=n_qkv),
        grid=(t // tm, n_qkv + 1),
        in_specs=[
            pl.BlockSpec((tm, d), lambda i, j: (i, 0)),
            pl.BlockSpec((1, d), lambda i, j: (0, 0)),
            pl.BlockSpec((d, c), lambda i, j: (0, j)),
        ],
        out_specs=[pl.BlockSpec((tm, c), lambda i, j: (i, jnp.minimum(j, n_qkv - 1))),
                   pl.BlockSpec((tm, c), lambda i, j: (i, 0))],
        out_shape=[jax.ShapeDtypeStruct((t, n_qkv * c), BF16), jax.ShapeDtypeStruct((t, c), F32)],
        scratch_shapes=[pltpu.VMEM((tm, d), BF16)],
        compiler_params=_params("parallel", "arbitrary"),
        name="inproj",
    )(h, norm_g.reshape(1, d), w_in)


def _ssm_discretise(log_dt, a_re, a_im, b_re, b_im):
    dt = jnp.exp(log_dt)[:, None]
    mag = jnp.exp(a_re * dt)
    lb_re = mag * jnp.cos(a_im * dt)
    lb_im = mag * jnp.sin(a_im * dt)
    den = a_re * a_re + a_im * a_im
    nr = lb_re - 1.0
    ni = lb_im
    f_re = (nr * a_re + ni * a_im) / den
    f_im = (ni * a_re - nr * a_im) / den
    bb_re = f_re[..., None] * b_re - f_im[..., None] * b_im
    bb_im = f_re[..., None] * b_im + f_im[..., None] * b_re
    return lb_re, lb_im, bb_re, bb_im


def _ssm_ops_kernel(pwr, pwi, bba, bbb, cca, crr, cii, krev_ref, w_ref, vre_ref, vim_ref, *, lc):
    cmul = lambda k, a, b: pwr[0, k:k + 1, :] * a + pwi[0, k:k + 1, :] * b
    w = jnp.concatenate([cmul(lc - 1 - tau, bba[0], bbb[0]) for tau in range(lc)], axis=0)
    w_ref[0] = w.astype(BF16)
    vre_ref[0] = jnp.concatenate([cmul(t + 1, crr[0], -cii[0]) for t in range(lc)], axis=0).astype(BF16)
    vim_ref[0] = jnp.concatenate([cmul(t + 1, -cii[0], -crr[0]) for t in range(lc)], axis=0).astype(BF16)
    krev_ref[0] = lax.dot_general(cca[0], w, (((1,), (1,)), ((), ())), precision=lax.Precision.HIGHEST,
                                  preferred_element_type=F32)


def _toeplitz_kernel(krev_ref, m_ref, *, group, lc):
    t = pl.program_id(0)
    g, _, n = krev_ref.shape
    row = lax.broadcasted_iota(jnp.int32, (n, n), 0)
    col = lax.broadcasted_iota(jnp.int32, (n, n), 1)
    shift = (row == col + group * (lc - 1 - t)).astype(BF16)
    krev = krev_ref[...].reshape(g * group, n).astype(BF16)
    res = jnp.dot(krev, shift, preferred_element_type=F32)
    m_ref[...] = res.reshape(g, group, n).astype(BF16)


SSM_HOP = 4


def _lag_rows(tau, n, lc):
    return pl.ds((tau % SSM_HOP) * (n // SSM_HOP) + tau // SSM_HOP, n // lc, stride=lc // SSM_HOP)


def _ssm_pack_kernel(u_ref, o_ref, hop_ref, *, lc):
    n = u_ref.shape[0]
    n_chunks = n // lc
    for dst, rows in _regroup(n, 1, SSM_HOP):
        hop_ref[dst, :] = u_ref[rows, :]
    for tau in range(lc):
        x = hop_ref[_lag_rows(tau, n, lc), :]
        o_ref[:, tau * n_chunks:(tau + 1) * n_chunks] = x.T.astype(o_ref.dtype)


def _ssm_unpack_kernel(y_ref, o_ref, hop_ref, *, lc):
    n = o_ref.shape[0]
    n_chunks = n // lc
    for t in range(lc):
        hop_ref[_lag_rows(t, n, lc), :] = y_ref[:, t * n_chunks:(t + 1) * n_chunks].T
    for src, rows in _regroup(n, 1, SSM_HOP):
        o_ref[rows, :] = hop_ref[src, :]


def _ssm_main_kernel(ut_ref, m_ref, w_ref, vre_ref, vim_ref, ar_ref, ai_ref, yt_ref, *, batch, lc, pack, n_steps):
    group = ut_ref.shape[0]
    seq = ut_ref.shape[1] // batch
    blocks = seq // (pack * lc)
    n_chunks = blocks * pack
    lanes = w_ref.shape[2]
    half = lanes // 2
    nt = (((1,), (1,)), ((), ()))
    chunk = lax.broadcasted_iota(jnp.int32, (n_chunks, lanes), 0)
    first = lax.broadcasted_iota(jnp.int32, (n_chunks, lanes), 1) < half
    first_v = lax.broadcasted_iota(jnp.int32, vre_ref.shape[1:], 1) < half
    lane0 = lambda b, blk, k: b * seq + blk * (pack * lc) + k * pack

    def load_xt(b):
        return jnp.concatenate(
            [jnp.concatenate([ut_ref[:, pl.ds(lane0(b, blk, tau), pack)] for blk in range(blocks)], axis=1)
             for tau in range(lc)], axis=0)

    for b0 in range(0, batch, 2):
        xts = [load_xt(b0), load_xt(b0 + 1)]
        sa, sb = [lax.dot_general(xt, w_ref[0], (((0,), (0,)), ((), ())), preferred_element_type=F32) for xt in xts]
        s_re = jnp.where(first, sa, pltpu.roll(sb, half, 1))
        s_im = jnp.where(first, pltpu.roll(sa, half, 1), sb)
        for j in range(n_steps):
            k = 1 << j
            shift = lambda x: jnp.where(chunk >= k, pltpu.roll(x, k, 0), 0.0)
            sh_re, sh_im = shift(s_re), shift(s_im)
            a_re, a_im = ar_ref[0, j:j + 1, :], ai_ref[0, j:j + 1, :]
            s_re, s_im = s_re + a_re * sh_re - a_im * sh_im, s_im + a_re * sh_im + a_im * sh_re
        enter = lambda x: jnp.where(chunk >= 1, pltpu.roll(x, 1, 0), 0.0).astype(BF16)
        in_re, in_im = enter(s_re), enter(s_im)
        for i, xt in enumerate(xts):
            own = first_v if i == 0 else ~first_v
            keep = lambda v: jnp.where(own, v, jnp.zeros_like(v))
            yt = (jnp.dot(m_ref[0], xt, preferred_element_type=F32)
                  + lax.dot_general(keep(vre_ref[0]), in_re, nt, preferred_element_type=F32)
                  + lax.dot_general(keep(vim_ref[0]), in_im, nt, preferred_element_type=F32))
            for t in range(lc):
                for blk in range(blocks):
                    yt_ref[:, pl.ds(lane0(b0 + i, blk, t), pack)] = (
                        yt[t * group:(t + 1) * group, blk * pack:(blk + 1) * pack])


def _ssm_conv(u, batch, log_dt, a_re, a_im, b_re, b_im, c_re, c_im):
    t, d_ssm = u.shape
    g, p = a_re.shape
    h = b_re.shape[-1]
    lc = SSM_CHUNK
    n = lc * h
    seq = t // batch
    pack = V7X_LANES
    pb = pack * lc
    n_steps = max(1, (seq // lc - 1).bit_length())
    assert seq % pb == 0 and d_ssm % V7X_LANES == 0 and batch % 2 == 0

    lb_re, lb_im, bb_re, bb_im = _ssm_discretise(log_dt, a_re, a_im, b_re, b_im)
    pw_r, pw_i = [jnp.ones_like(lb_re)], [jnp.zeros_like(lb_re)]
    for _ in range(lc):
        pw_r, pw_i = (pw_r + [pw_r[-1] * lb_re - pw_i[-1] * lb_im],
                      pw_i + [pw_r[-1] * lb_im + pw_i[-1] * lb_re])
    pw_r = jnp.stack(pw_r, axis=1)
    pw_i = jnp.stack(pw_i, axis=1)
    pair = lambda a, b: jnp.concatenate([a, b], axis=-1)
    bt_r, bt_i = bb_re.transpose(0, 2, 1), bb_im.transpose(0, 2, 1)
    tables = [pair(pw_r, pw_r), pair(pw_i, pw_i), pair(bt_r, bt_i), pair(-bt_i, bt_r),
              pair(c_re, -c_im), pair(c_re, c_re), pair(c_im, c_im)]
    sr, si = pw_r[:, lc], pw_i[:, lc]
    ar, ai = [], []
    for _ in range(n_steps):
        ar.append(pair(sr, sr))
        ai.append(pair(si, si))
        sr, si = sr * sr - si * si, 2.0 * sr * si
    ar = jnp.stack(ar, axis=1)
    ai = jnp.stack(ai, axis=1)

    per_group = lambda *shape: pl.BlockSpec((1,) + shape, lambda i: (i,) + (0,) * len(shape))
    krev, w, vre, vim = pl.pallas_call(
        functools.partial(_ssm_ops_kernel, lc=lc),
        grid=(g,),
        in_specs=[per_group(lc + 1, 2 * p)] * 2 + [per_group(h, 2 * p)] * 5,
        out_specs=[per_group(h, n)] + [per_group(n, 2 * p)] * 3,
        out_shape=[jax.ShapeDtypeStruct((g, h, n), F32)] + [jax.ShapeDtypeStruct((g, n, 2 * p), BF16)] * 3,
        compiler_params=_params("parallel"),
        name="ssm_ops",
    )(*tables)

    m = pl.pallas_call(
        functools.partial(_toeplitz_kernel, group=h, lc=lc),
        grid=(lc,),
        in_specs=[pl.BlockSpec((g, h, n), lambda i: (0, 0, 0))],
        out_specs=pl.BlockSpec((g, None, h, n), lambda i: (0, i, 0, 0)),
        out_shape=jax.ShapeDtypeStruct((g, lc, h, n), BF16),
        compiler_params=_params("parallel"),
        name="ssm_toeplitz",
    )(krev).reshape(g, n, n)

    tok_blk = pl.BlockSpec((pb, V7X_LANES), lambda i, j: (i, j))
    ch_blk = pl.BlockSpec((V7X_LANES, pb), lambda i, j: (j, i))
    relayout_grid = (t // pb, d_ssm // V7X_LANES)
    ut = pl.pallas_call(
        functools.partial(_ssm_pack_kernel, lc=lc),
        grid=relayout_grid,
        in_specs=[tok_blk],
        out_specs=ch_blk,
        out_shape=jax.ShapeDtypeStruct((d_ssm, t), BF16),
        scratch_shapes=[pltpu.VMEM((pb, V7X_LANES), F32)],
        compiler_params=_params("parallel", "parallel"),
        name="ssm_pack",
    )(u)
    yt = pl.pallas_call(
        functools.partial(_ssm_main_kernel, batch=batch, lc=lc, pack=pack, n_steps=n_steps),
        grid=(g,),
        in_specs=[pl.BlockSpec((h, t), lambda i: (i, 0)), per_group(n, n)] + [per_group(n, 2 * p)] * 3
        + [per_group(n_steps, 2 * p)] * 2,
        out_specs=pl.BlockSpec((h, t), lambda i: (i, 0)),
        out_shape=jax.ShapeDtypeStruct((d_ssm, t), F32),
        compiler_params=_params("parallel"),
        name="ssm_main",
    )(ut, m, w, vre, vim, ar, ai)
    return pl.pallas_call(
        functools.partial(_ssm_unpack_kernel, lc=lc),
        grid=relayout_grid,
        in_specs=[ch_blk],
        out_specs=tok_blk,
        out_shape=jax.ShapeDtypeStruct((t, d_ssm), F32),
        scratch_shapes=[pltpu.VMEM((pb, V7X_LANES), F32)],
        compiler_params=_params("parallel", "parallel"),
        name="ssm_unpack",
    )(yt)


def _regroup(n, dil, hop, piece=None):
    rows = n // (dil * hop)
    piece = piece or rows
    for res in range(dil * hop):
        a, r = divmod(res, dil)
        for k in range(0, rows, piece):
            yield (slice(res * rows + k, res * rows + k + piece),
                   pl.ds(r * (n // dil) + a + k * hop, piece, stride=hop))


def _attn_kernel(q_ref, k_ref, v_ref, o_ref, stage, kp4, vp4, kp16, vp16, qf, qp, ob, mb, lb, *, span, blk, unroll):
    s = pl.program_id(2)
    seq, lanes = k_ref.shape[1], k_ref.shape[2]
    dils = [d for _, d in DILATIONS]
    hop = dils[1]
    k_perm = {dils[1]: kp4, dils[2]: kp16}
    v_perm = {dils[1]: vp4, dils[2]: vp16}

    @pl.when(s == 0)
    def _():
        for src, perm in ((k_ref, k_perm), (v_ref, v_perm)):
            stage[0] = src[0].astype(F32)
            for dst, rows in _regroup(seq, dils[0], hop):
                x = stage[0, rows, :]
                stage[1, dst, :] = x
                perm[dils[1]][dst, :] = x.astype(BF16)
            for dst, rows in _regroup(seq, dils[1], hop):
                perm[dils[2]][dst, :] = stage[1, rows, :].astype(BF16)

    scale = 1.4426950408889634 / (HEAD_DIM ** 0.5)
    q0 = q_ref[0].astype(F32) * scale
    qf[0] = q0
    qp[0] = q0.astype(BF16)
    for dst, rows in _regroup(span, dils[0], hop):
        x = qf[0, rows, :]
        qf[1, dst, :] = x
        qp[1, dst, :] = x.astype(BF16)
    for dst, rows in _regroup(span, dils[1], hop):
        qp[2, dst, :] = qf[1, rows, :].astype(BF16)

    head0 = lax.broadcasted_iota(jnp.int32, (blk, lanes), 1) < HEAD_DIM
    head0_kv = lax.broadcasted_iota(jnp.int32, (2 * blk, lanes), 1) < HEAD_DIM
    rel =(lax.broadcasted_iota(jnp.int32, (blk, 2 * blk), 0)
           - lax.broadcasted_iota(jnp.int32, (blk, 2 * blk), 1))

    def run_branches(first_span):
        band = None if first_span else (rel >= -blk) & (rel <= 0)
        for di, dil in enumerate(dils):
            per_res = span // dil
            n_blk = per_res // blk
            res_len = seq // dil

            def body(it, carry, di=di, dil=dil, per_res=per_res, n_blk=n_blk, res_len=res_len):
                res = it // n_blk
                jj = it % n_blk
                qb = s * per_res + jj * blk
                kb = jnp.maximum(qb - blk, 0)
                row0 = pl.multiple_of(res * per_res + jj * blk, blk)
                koff = pl.multiple_of(res * res_len + kb, blk)
                q = qp[di, pl.ds(row0, blk), :]
                if dil == 1:
                    kk = k_ref[0, pl.ds(koff, 2 * blk), :]
                    vv = v_ref[0, pl.ds(koff, 2 * blk), :]
                else:
                    kk = k_perm[dil][pl.ds(koff, 2 * blk), :]
                    vv = v_perm[dil][pl.ds(koff, 2 * blk), :]
                if first_span:
                    lo = kb - qb
                    valid = (rel >= lo) & (rel <= lo + blk)
                else:
                    valid = band
                outs, maxs = [], []
                for head_mask, kv_mask in ((head0, head0_kv), (~head0, ~head0_kv)):
                    qh = jnp.where(head_mask, q, jnp.zeros_like(q))
                    sc = lax.dot_general(qh, kk, (((1,), (1,)), ((), ())), preferred_element_type=F32)
                    sc = jnp.where(valid, sc, NEG_INF)
                    m = jnp.max(sc, axis=-1, keepdims=True)
                    p = jnp.exp2(sc - m)
                    maxs.append(m)
                    outs.append(jnp.dot(p.astype(BF16), jnp.where(kv_mask, vv, jnp.ones_like(vv)),
                                        preferred_element_type=F32))
                ob[di, pl.ds(row0, blk), :] = jnp.where(head0, outs[0], outs[1])
                mb[di, pl.ds(row0, blk), :] = jnp.where(head0, maxs[0], maxs[1])
                lb[di, pl.ds(row0, blk), :] = jnp.where(head0, outs[1], outs[0])
                return carry

            lax.fori_loop(0, dil * n_blk, body, 0, unroll=unroll)

    pl.when(s == 0)(functools.partial(run_branches, True))
    pl.when(s != 0)(functools.partial(run_branches, False))

    unswap = lambda x: pltpu.roll(x, HEAD_DIM, 1)
    piece = 32
    last = len(dils) - 1
    for di in range(last, 0, -1):
        for src, rows in _regroup(span, dils[di - 1], hop, piece):
            m_a = mb[di - 1, rows, :]
            m_b = mb[di, src, :]
            l_b = lb[di, src, :]
            if di == last:
                l_b = unswap(l_b)
            m_new = jnp.maximum(m_a, m_b)
            w_a = jnp.exp2(m_a - m_new)
            w_b = jnp.exp2(m_b - m_new)
            ob[di - 1, rows, :] = ob[di - 1, rows, :] * w_a + ob[di, src, :] * w_b
            lb[di - 1, rows, :] = unswap(lb[di - 1, rows, :]) * w_a + l_b * w_b
            mb[di - 1, rows, :] = m_new
    o_ref[0] = (ob[0] / lb[0]).astype(o_ref.dtype)


def _attention(qkv, batch, *, unroll=16):
    t, c = qkv.shape[0], qkv.shape[1] // 3
    seq = t // batch
    span, blk = ATTN_SPAN, ATTN_BLOCK
    dils = [d for _, d in DILATIONS]
    assert dils == [1, 4, 16] and all(w // d == blk and span % (blk * d) == 0 for w, d in DILATIONS)
    assert seq % span == 0 and V7X_LANES % HEAD_DIM == 0
    qkv3 = qkv.reshape(batch, seq, 3 * c)
    lanes = V7X_LANES
    n_hp = c // lanes
    full = lambda part: pl.BlockSpec((1, seq, lanes), lambda b, hp, s: (b, 0, part * n_hp + hp))
    tile = pl.BlockSpec((1, span, lanes), lambda b, hp, s: (b, s, hp))
    out = pl.pallas_call(
        functools.partial(_attn_kernel, span=span, blk=blk, unroll=unroll),
        grid=(batch, n_hp, seq // span),
        in_specs=[tile, full(1), full(2)],
        out_specs=tile,
        out_shape=jax.ShapeDtypeStruct((batch, seq, c), BF16),
        scratch_shapes=[pltpu.VMEM((2, seq, lanes), F32)]
        + [pltpu.VMEM((seq, lanes), BF16)] * 4
        + [pltpu.VMEM((2, span, lanes), F32), pltpu.VMEM((len(dils), span, lanes), BF16)]
        + [pltpu.VMEM((len(dils), span, lanes), F32)] * 3,
        compiler_params=_params("parallel", "parallel", "arbitrary"),
        name="attn",
    )(qkv3, qkv3, qkv3)
    return out.reshape(t, c)


def _post_kernel(yc_ref, u_ref, att_ref, h_ref, d_ref, wglu_ref, bglu_ref, gs_ref, ga_ref, wo_ref, o_ref):
    y = jax.nn.gelu(yc_ref[...] + d_ref[...] * u_ref[...])
    gate = jax.nn.sigmoid(jnp.dot(y.astype(BF16), wglu_ref[...], preferred_element_type=F32) + bglu_ref[...])
    ys = _rmsnorm(y * gate, gs_ref[...]).astype(BF16)
    ya = _rmsnorm(att_ref[...].astype(F32), ga_ref[...]).astype(BF16)
    d_ssm = ys.shape[1]
    o_ref[...] = (h_ref[...]
                  + jnp.dot(ys, wo_ref[:d_ssm, :], preferred_element_type=F32)
                  + jnp.dot(ya, wo_ref[d_ssm:, :], preferred_element_type=F32))


def _post(yc, u, att, h, d, w_glu, b_glu, g_ssm, g_att, w_out, *, tm=512):
    t, dm = h.shape
    c = yc.shape[1]
    row = lambda i: (i, 0)
    const = lambda i: (0, 0)
    return pl.pallas_call(
        _post_kernel,
        grid=(t // tm,),
        in_specs=[pl.BlockSpec((tm, c), row), pl.BlockSpec((tm, c), row), pl.BlockSpec((tm, c), row),
                  pl.BlockSpec((tm, dm), row),
                  pl.BlockSpec((1, c), const), pl.BlockSpec((c, c), const), pl.BlockSpec((1, c), const),
                  pl.BlockSpec((1, c), const), pl.BlockSpec((1, c), const), pl.BlockSpec((2 * c, dm), const)],
        out_specs=pl.BlockSpec((tm, dm), row),
        out_shape=jax.ShapeDtypeStruct((t, dm), F32),
        compiler_params=_params("parallel"),
        name="post",
    )(yc, u, att, h, d.reshape(1, c), w_glu, b_glu.reshape(1, c), g_ssm.reshape(1, c), g_att.reshape(1, c), w_out)


def kernel(x, ffn1_norm, ffn1_w_gate, ffn1_w_up, ffn1_w_down, mix_norm, w_in, ssm_log_dt, ssm_a_re, ssm_a_im, ssm_b_re, ssm_b_im, ssm_c_re, ssm_c_im, ssm_d, ssm_w_glu, ssm_b_glu, ssm_out_norm, attn_out_norm, w_out, ffn2_norm, ffn2_w_gate, ffn2_w_up, ffn2_w_down, final_norm):
    batch, seq, d_model = x.shape
    depth = ffn1_norm.shape[0]
    h = x.reshape(batch * seq, d_model)
    for l in range(depth):
        last = l == depth - 1
        ffn_w = lambda *ws: [w.astype(BF16) for w in ws]
        h = _ffn(h, ffn1_norm[l], *ffn_w(ffn1_w_gate[l], ffn1_w_up[l], ffn1_w_down[l]))
        qkv, u = _inproj(h, mix_norm[l], w_in[l].astype(BF16))
        yc = _ssm_conv(u, batch, ssm_log_dt[l], ssm_a_re[l], ssm_a_im[l], ssm_b_re[l], ssm_b_im[l],
                       ssm_c_re[l], ssm_c_im[l])
        att = _attention(qkv, batch)
        h = _post(yc, u, att, h, ssm_d[l], ssm_w_glu[l].astype(BF16), ssm_b_glu[l], ssm_out_norm[l],
                  attn_out_norm[l], w_out[l].astype(BF16))
        h = _ffn(h, ffn2_norm[l], *ffn_w(ffn2_w_gate[l], ffn2_w_up[l], ffn2_w_down[l]),
                 final_g=final_norm if last else None)
    return h.reshape(batch, seq, d_model)
```

```python
import functools
import math

import jax
import jax.numpy as jnp
from jax import lax
from jax.experimental import pallas as pl
from jax.experimental.pallas import tpu as pltpu

F32 = jnp.float32
BF16 = jnp.bfloat16

EPS = 1e-6
NEG_INF = -1e30
HEAD_DIM = 64
SSM_GROUP = 16
DILATIONS = ((128, 1), (512, 4), (2048, 16))

V7X_LANES = 128
V7X_VMEM_BYTES = 64 * 1024 * 1024
V7X_VMEM_COMPILER_RESERVE_BYTES = 6 * 1024 * 1024
VMEM_LIMIT_BYTES = V7X_VMEM_BYTES - V7X_VMEM_COMPILER_RESERVE_BYTES

FFN_TOKENS = 1024
FFN_HIDDEN = 512
FFN_BLOCKS_PER_STEP = 2
INPROJ_TOKENS = 1024
POST_TOKENS = 512
SSM_CHUNK = 32
SSM_OPS_GROUPS = 8
SSM_MAIN_GROUPS = 2
SSM_TOEPLITZ_ROWS = 4
SSM_HOP = 4
ATTN_SPAN = 2048
ATTN_BLOCK = 128
ATTN_MERGE_ROWS = 32


def _rmsnorm(x, g):
    return x * lax.rsqrt(jnp.mean(x * x, axis=-1, keepdims=True) + EPS) * g


def _params(*semantics):
    return pltpu.CompilerParams(dimension_semantics=semantics, vmem_limit_bytes=VMEM_LIMIT_BYTES)


def _ffn_kernel(*refs, n_f, n_chunks, overlap, final_norm):
    per_step = FFN_BLOCKS_PER_STEP
    h_hbm, g_ref = refs[:2]
    w_refs = [refs[2 + 3 * k:5 + 3 * k] for k in range(per_step)]
    rest = refs[2 + 3 * per_step:]
    fg_ref = rest[0] if final_norm else None
    o_hbm, acc, n_ref, in_sem, out_sem = rest[1:] if final_norm else rest
    i, j = pl.program_id(0), pl.program_id(1)
    n_blocks, n_steps = pl.num_programs(0), pl.num_programs(1)
    tm = acc.shape[1]
    slot = i % 2

    def fetch(blk, s):
        return pltpu.make_async_copy(h_hbm.at[pl.ds(blk * tm, tm), :], acc.at[s], in_sem.at[s])

    def write_back(blk, s):
        return pltpu.make_async_copy(acc.at[s], o_hbm.at[pl.ds(blk * tm, tm), :], out_sem.at[s])

    @pl.when(j == 0)
    def _():
        @pl.when(i == 0)
        def _():
            fetch(0, 0).start()
        fetch(i, slot).wait()
        n_ref[...] = _rmsnorm(acc[slot], g_ref[...]).astype(BF16)

    @pl.when(j == 1)
    def _():
        @pl.when(i >= 1)
        def _():
            write_back(i - 1, 1 - slot).wait()

        @pl.when(i + 1 < n_blocks)
        def _():
            fetch(i + 1, 1 - slot).start()

    def hidden_block(wg_ref, wu_ref, wd_ref, masked):
        n = n_ref[...]
        gate = jnp.dot(n, wg_ref[...], preferred_element_type=F32)
        up = jnp.dot(n, wu_ref[...], preferred_element_type=F32)
        a = 0.5 * (gate * jax.nn.sigmoid(gate)) * up
        if masked:
            a = jnp.where(lax.broadcasted_iota(jnp.int32, a.shape, 1) >= overlap, a, 0.0)
        a = a.astype(BF16)
        cw = acc.shape[2] // n_chunks
        for c in range(n_chunks):
            sl = slice(c * cw, (c + 1) * cw)
            acc[slot, :, sl] += jnp.dot(a, wd_ref[:, sl], preferred_element_type=F32)

    tail = n_f - per_step * (n_f // per_step) or per_step

    @pl.when(j < n_steps - 1)
    def _():
        for k in range(per_step):
            hidden_block(*w_refs[k], masked=False)

    @pl.when(j == n_steps - 1)
    def _():
        for k in range(tail):
            hidden_block(*w_refs[k], masked=bool(overlap) and k == tail - 1)
        if final_norm:
            acc[slot] = _rmsnorm(acc[slot], fg_ref[...])
        write_back(i, slot).start()

        @pl.when(i == n_blocks - 1)
        def _():
            write_back(i, slot).wait()


def _ffn(h, norm_g, wg, wu, wd, final_g=None, *, tm=FFN_TOKENS, tf=FFN_HIDDEN):
    t, d = h.shape
    f = wg.shape[1]
    per_step = FFN_BLOCKS_PER_STEP
    n_f = pl.cdiv(f, tf)
    n_steps = pl.cdiv(n_f, per_step)
    overlap = n_f * tf - f
    assert f >= tf and overlap % V7X_LANES == 0 and n_steps >= 2 and t % tm == 0
    final_norm = final_g is not None

    def start(k):
        last_k = n_f - 1 - (n_f - 1 - k) % per_step
        return lambda j: pl.multiple_of(jnp.minimum(jnp.minimum(per_step * j + k, last_k) * tf, f - tf), V7X_LANES)

    in_specs = [pl.BlockSpec(memory_space=pl.ANY), pl.BlockSpec((1, d), lambda i, j: (0, 0))]
    args = [h, norm_g.reshape(1, d)]
    for k in range(per_step):
        in_specs += [
            pl.BlockSpec((pl.Element(d), pl.Element(tf)), lambda i, j, s=start(k): (0, s(j))),
            pl.BlockSpec((pl.Element(d), pl.Element(tf)), lambda i, j, s=start(k): (0, s(j))),
            pl.BlockSpec((pl.Element(tf), pl.Element(d)), lambda i, j, s=start(k): (s(j), 0)),
        ]
        args += [wg, wu, wd]
    if final_norm:
        in_specs.append(pl.BlockSpec((1, d), lambda i, j: (0, 0)))
        args.append(final_g.reshape(1, d))
    return pl.pallas_call(
        functools.partial(_ffn_kernel, n_f=n_f, n_chunks=4, overlap=overlap, final_norm=final_norm),
        grid=(t // tm, n_steps),
        in_specs=in_specs,
        out_specs=pl.BlockSpec(memory_space=pl.ANY),
        out_shape=jax.ShapeDtypeStruct((t, d), F32),
        scratch_shapes=[pltpu.VMEM((2, tm, d), F32), pltpu.VMEM((tm, d), BF16),
                        pltpu.SemaphoreType.DMA((2,)), pltpu.SemaphoreType.DMA((2,))],
        compiler_params=_params("arbitrary", "arbitrary"),
        name="ffn_final" if final_norm else "ffn",
    )(*args)


def _inproj_kernel(h_ref, g_ref, w_ref, qkv_ref, u_ref, n_ref, *, n_qkv):
    j = pl.program_id(1)

    @pl.when(j == 0)
    def _():
        n_ref[...] = _rmsnorm(h_ref[...], g_ref[...]).astype(BF16)

    @pl.when(j < n_qkv)
    def _():
        qkv_ref[...] = jnp.dot(n_ref[...], w_ref[...], preferred_element_type=F32).astype(qkv_ref.dtype)

    @pl.when(j == n_qkv)
    def _():
        u_ref[...] = jnp.dot(n_ref[...], w_ref[...], preferred_element_type=F32)


def _inproj(h, norm_g, w_in, *, tm=INPROJ_TOKENS):
    t, d = h.shape
    n_qkv = 3
    c = w_in.shape[1] // (n_qkv + 1)
    return pl.pallas_call(
        functools.partial(_inproj_kernel, n_qkv=n_qkv),
        grid=(t // tm, n_qkv + 1),
        in_specs=[
            pl.BlockSpec((tm, d), lambda i, j: (i, 0)),
            pl.BlockSpec((1, d), lambda i, j: (0, 0)),
            pl.BlockSpec((d, c), lambda i, j: (0, j)),
        ],
        out_specs=[pl.BlockSpec((tm, c), lambda i, j: (i, jnp.minimum(j, n_qkv - 1))),
                   pl.BlockSpec((tm, c), lambda i, j: (i, 0))],
        out_shape=[jax.ShapeDtypeStruct((t, n_qkv * c), BF16), jax.ShapeDtypeStruct((t, c), F32)],
        scratch_shapes=[pltpu.VMEM((tm, d), BF16)],
        compiler_params=_params("parallel", "arbitrary"),
        name="inproj",
    )(h, norm_g.reshape(1, d), w_in)


def _ssm_discretise(log_dt, a_re, a_im, b_re, b_im):
    dt = jnp.exp(log_dt)[:, None]
    mag = jnp.exp(a_re * dt)
    lb_re = mag * jnp.cos(a_im * dt)
    lb_im = mag * jnp.sin(a_im * dt)
    den = a_re * a_re + a_im * a_im
    nr = lb_re - 1.0
    ni = lb_im
    f_re = (nr * a_re + ni * a_im) / den
    f_im = (ni * a_re - nr * a_im) / den
    bb_re = f_re[..., None] * b_re - f_im[..., None] * b_im
    bb_im = f_re[..., None] * b_im + f_im[..., None] * b_re
    return lb_re, lb_im, bb_re, bb_im


def _ssm_ops_kernel(pwr, pwi, bba, bbb, cca, crr, cii, krev_ref, w_ref, vre_ref, vim_ref, *, lc):
    for gi in range(pwr.shape[0]):
        cmul = lambda k, a, b, gi=gi: pwr[gi, k:k + 1, :] * a + pwi[gi, k:k + 1, :] * b
        w = jnp.concatenate([cmul(lc - 1 - tau, bba[gi], bbb[gi]) for tau in range(lc)], axis=0)
        w_ref[gi] = w.astype(BF16)
        vre_ref[gi] = jnp.concatenate([cmul(t + 1, crr[gi], -cii[gi]) for t in range(lc)], axis=0).astype(BF16)
        vim_ref[gi] = jnp.concatenate([cmul(t + 1, -cii[gi], -crr[gi]) for t in range(lc)], axis=0).astype(BF16)
        krev_ref[gi] = lax.dot_general(cca[gi], w, (((1,), (1,)), ((), ())), precision=lax.Precision.HIGHEST,
                                       preferred_element_type=F32).astype(krev_ref.dtype)


def _toeplitz_kernel(krev_ref, m_ref, *, group, lc):
    g, _, n = krev_ref.shape
    krev = krev_ref[...].reshape(g * group, n)
    diag = lax.broadcasted_iota(jnp.int32, (n, n), 0) - lax.broadcasted_iota(jnp.int32, (n, n), 1)
    for k in range(m_ref.shape[1]):
        t = pl.program_id(0) * m_ref.shape[1] + k
        shift = (diag == group * (lc - 1 - t)).astype(BF16)
        res = jnp.dot(krev, shift, preferred_element_type=F32)
        m_ref[:, k] = res.reshape(g, group, n).astype(BF16)


def _lag_rows(tau, n, lc):
    return pl.ds((tau % SSM_HOP) * (n // SSM_HOP) + tau // SSM_HOP, n // lc, stride=lc // SSM_HOP)


def _ssm_pack_kernel(u_ref, o_ref, hop_ref, *, lc):
    n = u_ref.shape[0]
    n_chunks = n // lc
    for dst, rows in _regroup(n, 1, SSM_HOP):
        hop_ref[dst, :] = u_ref[rows, :]
    for tau in range(lc):
        x = hop_ref[_lag_rows(tau, n, lc), :]
        o_ref[:, tau * n_chunks:(tau + 1) * n_chunks] = x.T.astype(o_ref.dtype)


def _ssm_unpack_kernel(y_ref, o_ref, hop_ref, *, lc):
    n = o_ref.shape[0]
    n_chunks = n // lc
    for t in range(lc):
        hop_ref[_lag_rows(t, n, lc), :] = y_ref[:, t * n_chunks:(t + 1) * n_chunks].T
    for src, rows in _regroup(n, 1, SSM_HOP):
        o_ref[rows, :] = hop_ref[src, :]


def _ssm_main_kernel(ut_ref, m_ref, w_ref, vre_ref, vim_ref, ar_ref, ai_ref, yt_ref, *, batch, lc, pack, n_steps):
    n_groups = m_ref.shape[0]
    group = ut_ref.shape[0] // n_groups
    seq = ut_ref.shape[1] // batch
    blocks = seq // (pack * lc)
    n_chunks = blocks * pack
    lanes = w_ref.shape[2]
    half = lanes // 2
    nt = (((1,), (1,)), ((), ()))
    chunk = lax.broadcasted_iota(jnp.int32, (n_chunks, lanes), 0)
    first = lax.broadcasted_iota(jnp.int32, (n_chunks, lanes), 1) < half
    first_v = lax.broadcasted_iota(jnp.int32, vre_ref.shape[1:], 1) < half
    lane0 = lambda b, blk, k: b * seq + blk * (pack * lc) + k * pack

    for gi in range(n_groups):
        rows = slice(gi * group, (gi + 1) * group)

        def load_xt(b, rows=rows):
            return jnp.concatenate(
                [jnp.concatenate([ut_ref[rows, pl.ds(lane0(b, blk, tau), pack)] for blk in range(blocks)], axis=1)
                 for tau in range(lc)], axis=0)

        for b0 in range(0, batch, 2):
            xts = [load_xt(b0), load_xt(b0 + 1)]
            sa, sb = [lax.dot_general(xt, w_ref[gi], (((0,), (0,)), ((), ())), preferred_element_type=F32)
                      for xt in xts]
            s_re = jnp.where(first, sa, pltpu.roll(sb, half, 1))
            s_im = jnp.where(first, pltpu.roll(sa, half, 1), sb)
            for j in range(n_steps):
                k = 1 << j
                shift = lambda x: jnp.where(chunk >= k, pltpu.roll(x, k, 0), 0.0)
                sh_re, sh_im = shift(s_re), shift(s_im)
                a_re, a_im = ar_ref[gi, j:j + 1, :], ai_ref[gi, j:j + 1, :]
                s_re, s_im = s_re + a_re * sh_re - a_im * sh_im, s_im + a_re * sh_im + a_im * sh_re
            enter = lambda x: jnp.where(chunk >= 1, pltpu.roll(x, 1, 0), 0.0).astype(BF16)
            in_re, in_im = enter(s_re), enter(s_im)
            for i, xt in enumerate(xts):
                own = first_v if i == 0 else ~first_v
                keep = lambda v: jnp.where(own, v, jnp.zeros_like(v))
                yt = (jnp.dot(m_ref[gi], xt, preferred_element_type=F32)
                      + lax.dot_general(keep(vre_ref[gi]), in_re, nt, preferred_element_type=F32)
                      + lax.dot_general(keep(vim_ref[gi]), in_im, nt, preferred_element_type=F32))
                for t in range(lc):
                    for blk in range(blocks):
                        yt_ref[rows, pl.ds(lane0(b0 + i, blk, t), pack)] = (
                            yt[t * group:(t + 1) * group, blk * pack:(blk + 1) * pack])


def _ssm_conv(u, batch, log_dt, a_re, a_im, b_re, b_im, c_re, c_im):
    t, d_ssm = u.shape
    g, p = a_re.shape
    h = b_re.shape[-1]
    lc = SSM_CHUNK
    n = lc * h
    seq = t // batch
    pack = V7X_LANES
    pb = pack * lc
    n_steps = max(1, (seq // lc - 1).bit_length())
    assert seq % pb == 0 and d_ssm % V7X_LANES == 0 and batch % 2 == 0

    lb_re, lb_im, bb_re, bb_im = _ssm_discretise(log_dt, a_re, a_im, b_re, b_im)
    pw_r, pw_i = [jnp.ones_like(lb_re)], [jnp.zeros_like(lb_re)]
    for _ in range(lc):
        pw_r, pw_i = (pw_r + [pw_r[-1] * lb_re - pw_i[-1] * lb_im],
                      pw_i + [pw_r[-1] * lb_im + pw_i[-1] * lb_re])
    pw_r = jnp.stack(pw_r, axis=1)
    pw_i = jnp.stack(pw_i, axis=1)
    pair = lambda a, b: jnp.concatenate([a, b], axis=-1)
    bt_r, bt_i = bb_re.transpose(0, 2, 1), bb_im.transpose(0, 2, 1)
    tables = [pair(pw_r, pw_r), pair(pw_i, pw_i), pair(bt_r, bt_i), pair(-bt_i, bt_r),
              pair(c_re, -c_im), pair(c_re, c_re), pair(c_im, c_im)]
    sr, si = pw_r[:, lc], pw_i[:, lc]
    ar, ai = [], []
    for _ in range(n_steps):
        ar.append(pair(sr, sr))
        ai.append(pair(si, si))
        sr, si = sr * sr - si * si, 2.0 * sr * si
    ar = jnp.stack(ar, axis=1)
    ai = jnp.stack(ai, axis=1)

    per_group = lambda *shape, gs=1: pl.BlockSpec((gs,) + shape, lambda i: (i,) + (0,) * len(shape))
    gs, gm = SSM_OPS_GROUPS, SSM_MAIN_GROUPS
    assert g % gs == 0 and g % gm == 0
    krev, w, vre, vim = pl.pallas_call(
        functools.partial(_ssm_ops_kernel, lc=lc),
        grid=(g // gs,),
        in_specs=[per_group(lc + 1, 2 * p, gs=gs)] * 2 + [per_group(h, 2 * p, gs=gs)] * 5,
        out_specs=[per_group(h, n, gs=gs)] + [per_group(n, 2 * p, gs=gs)] * 3,
        out_shape=[jax.ShapeDtypeStruct((g, h, n), BF16)] + [jax.ShapeDtypeStruct((g, n, 2 * p), BF16)] * 3,
        compiler_params=_params("parallel"),
        name="ssm_ops",
    )(*tables)

    m = pl.pallas_call(
        functools.partial(_toeplitz_kernel, group=h, lc=lc),
        grid=(lc // SSM_TOEPLITZ_ROWS,),
        in_specs=[pl.BlockSpec((g, h, n), lambda i: (0, 0, 0))],
        out_specs=pl.BlockSpec((g, SSM_TOEPLITZ_ROWS, h, n), lambda i: (0, i, 0, 0)),
        out_shape=jax.ShapeDtypeStruct((g, lc, h, n), BF16),
        compiler_params=_params("parallel"),
        name="ssm_toeplitz",
    )(krev).reshape(g, n, n)

    tok_blk = pl.BlockSpec((pb, V7X_LANES), lambda i, j: (i, j))
    ch_blk = pl.BlockSpec((V7X_LANES, pb), lambda i, j: (j, i))
    relayout_grid = (t // pb, d_ssm // V7X_LANES)
    ut = pl.pallas_call(
        functools.partial(_ssm_pack_kernel, lc=lc),
        grid=relayout_grid,
        in_specs=[tok_blk],
        out_specs=ch_blk,
        out_shape=jax.ShapeDtypeStruct((d_ssm, t), BF16),
        scratch_shapes=[pltpu.VMEM((pb, V7X_LANES), F32)],
        compiler_params=_params("parallel", "parallel"),
        name="ssm_pack",
    )(u)
    yt = pl.pallas_call(
        functools.partial(_ssm_main_kernel, batch=batch, lc=lc, pack=pack, n_steps=n_steps),
        grid=(g // gm,),
        in_specs=[pl.BlockSpec((gm * h, t), lambda i: (i, 0)), per_group(n, n, gs=gm)]
        + [per_group(n, 2 * p, gs=gm)] * 3 + [per_group(n_steps, 2 * p, gs=gm)] * 2,
        out_specs=pl.BlockSpec((gm * h, t), lambda i: (i, 0)),
        out_shape=jax.ShapeDtypeStruct((d_ssm, t), F32),
        compiler_params=_params("parallel"),
        name="ssm_main",
    )(ut, m, w, vre, vim, ar, ai)
    return pl.pallas_call(
        functools.partial(_ssm_unpack_kernel, lc=lc),
        grid=relayout_grid,
        in_specs=[ch_blk],
        out_specs=tok_blk,
        out_shape=jax.ShapeDtypeStruct((t, d_ssm), F32),
        scratch_shapes=[pltpu.VMEM((pb, V7X_LANES), F32)],
        compiler_params=_params("parallel", "parallel"),
        name="ssm_unpack",
    )(yt)


def _regroup(n, dil, hop, piece=None):
    rows = n // (dil * hop)
    piece = piece or rows
    for res in range(dil * hop):
        a, r = divmod(res, dil)
        for k in range(0, rows, piece):
            yield (slice(res * rows + k, res * rows + k + piece),
                   pl.ds(r * (n // dil) + a + k * hop, piece, stride=hop))


def _attn_kernel(q_ref, k_ref, v_ref, o_ref, stage, kp4, vp4, kp16, vp16, qf, qp, ob, mb, lb, *, span, blk):
    s = pl.program_id(2)
    seq, lanes = k_ref.shape[1], k_ref.shape[2]
    dils = [d for _, d in DILATIONS]
    hop = dils[1]
    k_perm = {dils[1]: kp4, dils[2]: kp16}
    v_perm = {dils[1]: vp4, dils[2]: vp16}

    @pl.when(s == 0)
    def _():
        for src, perm in ((k_ref, k_perm), (v_ref, v_perm)):
            stage[0] = src[0].astype(F32)
            for dst, rows in _regroup(seq, dils[0], hop):
                x = stage[0, rows, :]
                stage[1, dst, :] = x
                perm[dils[1]][dst, :] = x.astype(BF16)
            for dst, rows in _regroup(seq, dils[1], hop):
                perm[dils[2]][dst, :] = stage[1, rows, :].astype(BF16)

    scale = math.log2(math.e) / math.sqrt(HEAD_DIM)
    q0 = q_ref[0].astype(F32) * scale
    qf[0] = q0
    qp[0] = q0.astype(BF16)
    for dst, rows in _regroup(span, dils[0], hop):
        x = qf[0, rows, :]
        qf[1, dst, :] = x
        qp[1, dst, :] = x.astype(BF16)
    for dst, rows in _regroup(span, dils[1], hop):
        qp[2, dst, :] = qf[1, rows, :].astype(BF16)

    head0 = lax.broadcasted_iota(jnp.int32, (blk, lanes), 1) < HEAD_DIM
    head0_kv = lax.broadcasted_iota(jnp.int32, (2 * blk, lanes), 1) < HEAD_DIM
    rel =(lax.broadcasted_iota(jnp.int32, (blk, 2 * blk), 0)
           - lax.broadcasted_iota(jnp.int32, (blk, 2 * blk), 1))

    def run_branches(first_span):
        band = None if first_span else (rel >= -blk) & (rel <= 0)
        for di, dil in enumerate(dils):
            per_res = span // dil
            n_blk = per_res // blk
            res_len = seq // dil

            def body(it, carry, di=di, dil=dil, per_res=per_res, n_blk=n_blk, res_len=res_len):
                res = it // n_blk
                jj = it % n_blk
                qb = s * per_res + jj * blk
                kb = jnp.maximum(qb - blk, 0)
                row0 = pl.multiple_of(res * per_res + jj * blk, blk)
                koff = pl.multiple_of(res * res_len + kb, blk)
                q = qp[di, pl.ds(row0, blk), :]
                if dil == 1:
                    kk = k_ref[0, pl.ds(koff, 2 * blk), :]
                    vv = v_ref[0, pl.ds(koff, 2 * blk), :]
                else:
                    kk = k_perm[dil][pl.ds(koff, 2 * blk), :]
                    vv = v_perm[dil][pl.ds(koff, 2 * blk), :]
                if first_span:
                    lo = kb - qb
                    valid = (rel >= lo) & (rel <= lo + blk)
                else:
                    valid = band
                outs, maxs = [], []
                for head_mask, kv_mask in ((head0, head0_kv), (~head0, ~head0_kv)):
                    qh = jnp.where(head_mask, q, jnp.zeros_like(q))
                    sc = lax.dot_general(qh, kk, (((1,), (1,)), ((), ())), preferred_element_type=F32)
                    sc = jnp.where(valid, sc, NEG_INF)
                    m = jnp.max(sc, axis=-1, keepdims=True)
                    p = jnp.exp2(sc - m)
                    maxs.append(m)
                    outs.append(jnp.dot(p.astype(BF16), jnp.where(kv_mask, vv, jnp.ones_like(vv)),
                                        preferred_element_type=F32))
                ob[di, pl.ds(row0, blk), :] = jnp.where(head0, outs[0], outs[1])
                mb[di, pl.ds(row0, blk), :] = jnp.where(head0, maxs[0], maxs[1])
                lb[di, pl.ds(row0, blk), :] = jnp.where(head0, outs[1], outs[0])
                return carry

            lax.fori_loop(0, dil * n_blk, body, 0, unroll=True)

    pl.when(s == 0)(functools.partial(run_branches, True))
    pl.when(s != 0)(functools.partial(run_branches, False))

    unswap = lambda x: pltpu.roll(x, HEAD_DIM, 1)
    piece = ATTN_MERGE_ROWS
    last = len(dils) - 1
    for di in range(last, 0, -1):
        for src, rows in _regroup(span, dils[di - 1], hop, piece):
            m_a = mb[di - 1, rows, :]
            m_b = mb[di, src, :]
            l_b = lb[di, src, :]
            if di == last:
                l_b = unswap(l_b)
            m_new = jnp.maximum(m_a, m_b)
            w_a = jnp.exp2(m_a - m_new)
            w_b = jnp.exp2(m_b - m_new)
            ob[di - 1, rows, :] = ob[di - 1, rows, :] * w_a + ob[di, src, :] * w_b
            lb[di - 1, rows, :] = unswap(lb[di - 1, rows, :]) * w_a + l_b * w_b
            mb[di - 1, rows, :] = m_new
    o_ref[0] = (ob[0] / lb[0]).astype(o_ref.dtype)


def _attention(qkv, batch):
    t, c = qkv.shape[0], qkv.shape[1] // 3
    seq = t // batch
    span, blk = ATTN_SPAN, ATTN_BLOCK
    dils = [d for _, d in DILATIONS]
    assert dils == [1, 4, 16] and all(w // d == blk and span % (blk * d) == 0 for w, d in DILATIONS)
    assert seq % span == 0 and V7X_LANES % HEAD_DIM == 0
    qkv3 = qkv.reshape(batch, seq, 3 * c)
    lanes = V7X_LANES
    n_hp = c // lanes
    full = lambda part: pl.BlockSpec((1, seq, lanes), lambda b, hp, s: (b, 0, part * n_hp + hp))
    tile = pl.BlockSpec((1, span, lanes), lambda b, hp, s: (b, s, hp))
    out = pl.pallas_call(
        functools.partial(_attn_kernel, span=span, blk=blk),
        grid=(batch, n_hp, seq // span),
        in_specs=[tile, full(1), full(2)],
        out_specs=tile,
        out_shape=jax.ShapeDtypeStruct((batch, seq, c), BF16),
        scratch_shapes=[pltpu.VMEM((2, seq, lanes), F32)]
        + [pltpu.VMEM((seq, lanes), BF16)] * 4
        + [pltpu.VMEM((2, span, lanes), F32), pltpu.VMEM((len(dils), span, lanes), BF16)]
        + [pltpu.VMEM((len(dils), span, lanes), F32)] * 3,
        compiler_params=_params("parallel", "parallel", "arbitrary"),
        name="attn",
    )(qkv3, qkv3, qkv3)
    return out.reshape(t, c)


def _post_kernel(yc_ref, u_ref, att_ref, h_ref, d_ref, wglu_ref, bglu_ref, gs_ref, ga_ref, wo_ref, o_ref):
    y = jax.nn.gelu(yc_ref[...] + d_ref[...] * u_ref[...])
    gate = jax.nn.sigmoid(jnp.dot(y.astype(BF16), wglu_ref[...], preferred_element_type=F32) + bglu_ref[...])
    ys = _rmsnorm(y * gate, gs_ref[...]).astype(BF16)
    ya = _rmsnorm(att_ref[...].astype(F32), ga_ref[...]).astype(BF16)
    d_ssm = ys.shape[1]
    o_ref[...] = (h_ref[...]
                  + jnp.dot(ys, wo_ref[:d_ssm, :], preferred_element_type=F32)
                  + jnp.dot(ya, wo_ref[d_ssm:, :], preferred_element_type=F32))


def _post(yc, u, att, h, d, w_glu, b_glu, g_ssm, g_att, w_out, *, tm=POST_TOKENS):
    t, dm = h.shape
    c = yc.shape[1]
    row = lambda i: (i, 0)
    const = lambda i: (0, 0)
    return pl.pallas_call(
        _post_kernel,
        grid=(t // tm,),
        in_specs=[pl.BlockSpec((tm, c), row), pl.BlockSpec((tm, c), row), pl.BlockSpec((tm, c), row),
                  pl.BlockSpec((tm, dm), row),
                  pl.BlockSpec((1, c), const), pl.BlockSpec((c, c), const), pl.BlockSpec((1, c), const),
                  pl.BlockSpec((1, c), const), pl.BlockSpec((1, c), const), pl.BlockSpec((2 * c, dm), const)],
        out_specs=pl.BlockSpec((tm, dm), row),
        out_shape=jax.ShapeDtypeStruct((t, dm), F32),
        compiler_params=_params("parallel"),
        name="post",
    )(yc, u, att, h, d.reshape(1, c), w_glu, b_glu.reshape(1, c), g_ssm.reshape(1, c), g_att.reshape(1, c), w_out)


def kernel(x, ffn1_norm, ffn1_w_gate, ffn1_w_up, ffn1_w_down, mix_norm, w_in, ssm_log_dt, ssm_a_re, ssm_a_im, ssm_b_re, ssm_b_im, ssm_c_re, ssm_c_im, ssm_d, ssm_w_glu, ssm_b_glu, ssm_out_norm, attn_out_norm, w_out, ffn2_norm, ffn2_w_gate, ffn2_w_up, ffn2_w_down, final_norm):
    batch, seq, d_model = x.shape
    depth = ffn1_norm.shape[0]
    h = x.reshape(batch * seq, d_model)
    for l in range(depth):
        last = l == depth - 1
        ffn_w = lambda *ws: [w.astype(BF16) for w in ws]
        h = _ffn(h, ffn1_norm[l], *ffn_w(ffn1_w_gate[l], ffn1_w_up[l], ffn1_w_down[l]))
        qkv, u = _inproj(h, mix_norm[l], w_in[l].astype(BF16))
        yc = _ssm_conv(u, batch, ssm_log_dt[l], ssm_a_re[l], ssm_a_im[l], ssm_b_re[l], ssm_b_im[l],
                       ssm_c_re[l], ssm_c_im[l])
        att = _attention(qkv, batch)
        h = _post(yc, u, att, h, ssm_d[l], ssm_w_glu[l].astype(BF16), ssm_b_glu[l], ssm_out_norm[l],
                  attn_out_norm[l], w_out[l].astype(BF16))
        h = _ffn(h, ffn2_norm[l], *ffn_w(ffn2_w_gate[l], ffn2_w_up[l], ffn2_w_down[l]),
                 final_g=final_norm if last else None)
    return h.reshape(batch, seq, d_model)
```

```python
import functools
import math

import jax
import jax.numpy as jnp
from jax import lax
from jax.experimental import pallas as pl
from jax.experimental.pallas import tpu as pltpu

F32 = jnp.float32
BF16 = jnp.bfloat16

EPS = 1e-6
NEG_INF = -1e30
HEAD_DIM = 64
SSM_GROUP = 16
DILATIONS = ((128, 1), (512, 4), (2048, 16))

V7X_LANES = 128
V7X_VMEM_BYTES = 64 * 1024 * 1024
V7X_VMEM_COMPILER_RESERVE_BYTES = 6 * 1024 * 1024
VMEM_LIMIT_BYTES = V7X_VMEM_BYTES - V7X_VMEM_COMPILER_RESERVE_BYTES

FFN_TOKENS = 1024
FFN_HIDDEN = 512
FFN_BLOCKS_PER_STEP = 2
INPROJ_TOKENS = 1024
POST_TOKENS = 512
SSM_CHUNK = 32
SSM_OPS_GROUPS = 8
SSM_MAIN_GROUPS = 4
SSM_TOEPLITZ_ROWS = 8
SSM_HOP = 4
ATTN_SPAN = 2048
ATTN_BLOCK = 128
ATTN_MERGE_ROWS = 32


def _rmsnorm(x, g):
    return x * lax.rsqrt(jnp.mean(x * x, axis=-1, keepdims=True) + EPS) * g


def _params(*semantics):
    return pltpu.CompilerParams(dimension_semantics=semantics, vmem_limit_bytes=VMEM_LIMIT_BYTES)


def _ffn_kernel(*refs, n_f, n_chunks, overlap, final_norm):
    per_step = FFN_BLOCKS_PER_STEP
    h_hbm, g_ref = refs[:2]
    w_refs = [refs[2 + 3 * k:5 + 3 * k] for k in range(per_step)]
    rest = refs[2 + 3 * per_step:]
    fg_ref = rest[0] if final_norm else None
    o_hbm, acc, n_ref, in_sem, out_sem = rest[1:] if final_norm else rest
    i, j = pl.program_id(0), pl.program_id(1)
    n_blocks, n_steps = pl.num_programs(0), pl.num_programs(1)
    tm = acc.shape[1]
    slot = i % 2

    def fetch(blk, s):
        return pltpu.make_async_copy(h_hbm.at[pl.ds(blk * tm, tm), :], acc.at[s], in_sem.at[s])

    def write_back(blk, s):
        return pltpu.make_async_copy(acc.at[s], o_hbm.at[pl.ds(blk * tm, tm), :], out_sem.at[s])

    @pl.when(j == 1)
    def _():
        @pl.when(i >= 1)
        def _():
            write_back(i - 1, 1 - slot).wait()

        @pl.when(i + 1 < n_blocks)
        def _():
            fetch(i + 1, 1 - slot).start()

    def hidden_block(wg_ref, wu_ref, wd_ref, masked, n=None):
        n = n_ref[...] if n is None else n
        gate = jnp.dot(n, wg_ref[...], preferred_element_type=F32)
        up = jnp.dot(n, wu_ref[...], preferred_element_type=F32)
        a = 0.5 * (gate * jax.nn.sigmoid(gate)) * up
        if masked:
            a = jnp.where(lax.broadcasted_iota(jnp.int32, a.shape, 1) >= overlap, a, 0.0)
        a = a.astype(BF16)
        cw = acc.shape[2] // n_chunks
        for c in range(n_chunks):
            sl = slice(c * cw, (c + 1) * cw)
            acc[slot, :, sl] += jnp.dot(a, wd_ref[:, sl], preferred_element_type=F32)

    tail = n_f - per_step * (n_f // per_step) or per_step

    @pl.when(j == 0)
    def _():
        @pl.when(i == 0)
        def _():
            fetch(0, 0).start()
        fetch(i, slot).wait()
        n = _rmsnorm(acc[slot], g_ref[...]).astype(BF16)
        n_ref[...] = n
        for k in range(per_step):
            hidden_block(*w_refs[k], masked=False, n=n)

    @pl.when((j > 0) & (j < n_steps - 1))
    def _():
        for k in range(per_step):
            hidden_block(*w_refs[k], masked=False)

    @pl.when(j == n_steps - 1)
    def _():
        for k in range(tail):
            hidden_block(*w_refs[k], masked=bool(overlap) and k == tail - 1)
        if final_norm:
            acc[slot] = _rmsnorm(acc[slot], fg_ref[...])
        write_back(i, slot).start()

        @pl.when(i == n_blocks - 1)
        def _():
            write_back(i, slot).wait()


def _ffn(h, norm_g, wg, wu, wd, final_g=None, *, tm=FFN_TOKENS, tf=FFN_HIDDEN):
    t, d = h.shape
    f = wg.shape[1]
    per_step = FFN_BLOCKS_PER_STEP
    n_f = pl.cdiv(f, tf)
    n_steps = pl.cdiv(n_f, per_step)
    overlap = n_f * tf - f
    assert f >= tf and overlap % V7X_LANES == 0 and n_steps >= 2 and t % tm == 0
    final_norm = final_g is not None

    def start(k):
        last_k = n_f - 1 - (n_f - 1 - k) % per_step
        return lambda j: pl.multiple_of(jnp.minimum(jnp.minimum(per_step * j + k, last_k) * tf, f - tf), V7X_LANES)

    in_specs = [pl.BlockSpec(memory_space=pl.ANY), pl.BlockSpec((1, d), lambda i, j: (0, 0))]
    args = [h, norm_g.reshape(1, d)]
    for k in range(per_step):
        in_specs += [
            pl.BlockSpec((pl.Element(d), pl.Element(tf)), lambda i, j, s=start(k): (0, s(j))),
            pl.BlockSpec((pl.Element(d), pl.Element(tf)), lambda i, j, s=start(k): (0, s(j))),
            pl.BlockSpec((pl.Element(tf), pl.Element(d)), lambda i, j, s=start(k): (s(j), 0)),
        ]
        args += [wg, wu, wd]
    if final_norm:
        in_specs.append(pl.BlockSpec((1, d), lambda i, j: (0, 0)))
        args.append(final_g.reshape(1, d))
    return pl.pallas_call(
        functools.partial(_ffn_kernel, n_f=n_f, n_chunks=4, overlap=overlap, final_norm=final_norm),
        grid=(t // tm, n_steps),
        in_specs=in_specs,
        out_specs=pl.BlockSpec(memory_space=pl.ANY),
        out_shape=jax.ShapeDtypeStruct((t, d), F32),
        scratch_shapes=[pltpu.VMEM((2, tm, d), F32), pltpu.VMEM((tm, d), BF16),
                        pltpu.SemaphoreType.DMA((2,)), pltpu.SemaphoreType.DMA((2,))],
        compiler_params=_params("arbitrary", "arbitrary"),
        name="ffn_final" if final_norm else "ffn",
    )(*args)


def _inproj_kernel(h_ref, g_ref, w_ref, qkv_ref, u_ref, n_ref, *, n_qkv):
    j = pl.program_id(1)

    @pl.when(j == 0)
    def _():
        n_ref[...] = _rmsnorm(h_ref[...], g_ref[...]).astype(BF16)

    @pl.when(j < n_qkv)
    def _():
        qkv_ref[...] = jnp.dot(n_ref[...], w_ref[...], preferred_element_type=F32).astype(qkv_ref.dtype)

    @pl.when(j == n_qkv)
    def _():
        u_ref[...] = jnp.dot(n_ref[...], w_ref[...], preferred_element_type=F32)


def _inproj(h, norm_g, w_in, *, tm=INPROJ_TOKENS):
    t, d = h.shape
    n_qkv = 3
    c = w_in.shape[1] // (n_qkv + 1)
    return pl.pallas_call(
        functools.partial(_inproj_kernel, n_qkv=n_qkv),
        grid=(t // tm, n_qkv + 1),
        in_specs=[
            pl.BlockSpec((tm, d), lambda i, j: (i, 0)),
            pl.BlockSpec((1, d), lambda i, j: (0, 0)),
            pl.BlockSpec((d, c), lambda i, j: (0, j)),
        ],
        out_specs=[pl.BlockSpec((tm, c), lambda i, j: (i, jnp.minimum(j, n_qkv - 1))),
                   pl.BlockSpec((tm, c), lambda i, j: (i, 0))],
        out_shape=[jax.ShapeDtypeStruct((t, n_qkv * c), BF16), jax.ShapeDtypeStruct((t, c), F32)],
        scratch_shapes=[pltpu.VMEM((tm, d), BF16)],
        compiler_params=_params("parallel", "arbitrary"),
        name="inproj",
    )(h, norm_g.reshape(1, d), w_in)


def _ssm_discretise(log_dt, a_re, a_im, b_re, b_im):
    dt = jnp.exp(log_dt)[:, None]
    mag = jnp.exp(a_re * dt)
    lb_re = mag * jnp.cos(a_im * dt)
    lb_im = mag * jnp.sin(a_im * dt)
    den = a_re * a_re + a_im * a_im
    nr = lb_re - 1.0
    ni = lb_im
    f_re = (nr * a_re + ni * a_im) / den
    f_im = (ni * a_re - nr * a_im) / den
    bb_re = f_re[..., None] * b_re - f_im[..., None] * b_im
    bb_im = f_re[..., None] * b_im + f_im[..., None] * b_re
    return lb_re, lb_im, bb_re, bb_im


def _ssm_ops_kernel(pwr, pwi, bba, bbb, cca, crr, cii, krev_ref, w_ref, vre_ref, vim_ref, *, lc):
    for gi in range(pwr.shape[0]):
        cmul = lambda k, a, b, gi=gi: pwr[gi, k:k + 1, :] * a + pwi[gi, k:k + 1, :] * b
        w = jnp.concatenate([cmul(lc - 1 - tau, bba[gi], bbb[gi]) for tau in range(lc)], axis=0)
        w_ref[gi] = w.astype(BF16)
        vre_ref[gi] = jnp.concatenate([cmul(t + 1, crr[gi], -cii[gi]) for t in range(lc)], axis=0).astype(BF16)
        vim_ref[gi] = jnp.concatenate([cmul(t + 1, -cii[gi], -crr[gi]) for t in range(lc)], axis=0).astype(BF16)
        krev_ref[gi] = lax.dot_general(cca[gi], w, (((1,), (1,)), ((), ())), precision=lax.Precision.HIGHEST,
                                       preferred_element_type=F32).astype(krev_ref.dtype)


def _toeplitz_kernel(krev_ref, m_ref, *, group, lc):
    g, _, n = krev_ref.shape
    krev = krev_ref[...].reshape(g * group, n)
    diag = lax.broadcasted_iota(jnp.int32, (n, n), 0) - lax.broadcasted_iota(jnp.int32, (n, n), 1)
    for k in range(m_ref.shape[1]):
        t = pl.program_id(0) * m_ref.shape[1] + k
        shift = (diag == group * (lc - 1 - t)).astype(BF16)
        res = jnp.dot(krev, shift, preferred_element_type=F32)
        m_ref[:, k] = res.reshape(g, group, n).astype(BF16)


def _lag_rows(tau, n, lc):
    return pl.ds((tau % SSM_HOP) * (n // SSM_HOP) + tau // SSM_HOP, n // lc, stride=lc // SSM_HOP)


def _ssm_pack_kernel(u_ref, o_ref, hop_ref, *, lc):
    n = u_ref.shape[0]
    n_chunks = n // lc
    for dst, rows in _regroup(n, 1, SSM_HOP):
        hop_ref[dst, :] = u_ref[rows, :]
    for tau in range(lc):
        x = hop_ref[_lag_rows(tau, n, lc), :]
        o_ref[:, tau * n_chunks:(tau + 1) * n_chunks] = x.T.astype(o_ref.dtype)


def _ssm_unpack_kernel(y_ref, o_ref, hop_ref, *, lc):
    n = o_ref.shape[0]
    n_chunks = n // lc
    for t in range(lc):
        hop_ref[_lag_rows(t, n, lc), :] = y_ref[:, t * n_chunks:(t + 1) * n_chunks].T
    for src, rows in _regroup(n, 1, SSM_HOP):
        o_ref[rows, :] = hop_ref[src, :]


def _ssm_main_kernel(ut_ref, m_ref, w_ref, vre_ref, vim_ref, ar_ref, ai_ref, yt_ref, *, batch, lc, pack, n_steps):
    n_groups = m_ref.shape[0]
    group = ut_ref.shape[0] // n_groups
    seq = ut_ref.shape[1] // batch
    blocks = seq // (pack * lc)
    n_chunks = blocks * pack
    lanes = w_ref.shape[2]
    half = lanes // 2
    nt = (((1,), (1,)), ((), ()))
    chunk = lax.broadcasted_iota(jnp.int32, (n_chunks, lanes), 0)
    first = lax.broadcasted_iota(jnp.int32, (n_chunks, lanes), 1) < half
    first_v = lax.broadcasted_iota(jnp.int32, vre_ref.shape[1:], 1) < half
    lane0 = lambda b, blk, k: b * seq + blk * (pack * lc) + k * pack

    for gi in range(n_groups):
        rows = slice(gi * group, (gi + 1) * group)

        def load_xt(b, rows=rows):
            return jnp.concatenate(
                [jnp.concatenate([ut_ref[rows, pl.ds(lane0(b, blk, tau), pack)] for blk in range(blocks)], axis=1)
                 for tau in range(lc)], axis=0)

        for b0 in range(0, batch, 2):
            xts = [load_xt(b0), load_xt(b0 + 1)]
            sa, sb = [lax.dot_general(xt, w_ref[gi], (((0,), (0,)), ((), ())), preferred_element_type=F32)
                      for xt in xts]
            s_re = jnp.where(first, sa, pltpu.roll(sb, half, 1))
            s_im = jnp.where(first, pltpu.roll(sa, half, 1), sb)
            for j in range(n_steps):
                k = 1 << j
                shift = lambda x: jnp.where(chunk >= k, pltpu.roll(x, k, 0), 0.0)
                sh_re, sh_im = shift(s_re), shift(s_im)
                a_re, a_im = ar_ref[gi, j:j + 1, :], ai_ref[gi, j:j + 1, :]
                s_re, s_im = s_re + a_re * sh_re - a_im * sh_im, s_im + a_re * sh_im + a_im * sh_re
            enter = lambda x: jnp.where(chunk >= 1, pltpu.roll(x, 1, 0), 0.0).astype(BF16)
            in_re, in_im = enter(s_re), enter(s_im)
            for i, xt in enumerate(xts):
                own = first_v if i == 0 else ~first_v
                keep = lambda v: jnp.where(own, v, jnp.zeros_like(v))
                yt = (jnp.dot(m_ref[gi], xt, preferred_element_type=F32)
                      + lax.dot_general(keep(vre_ref[gi]), in_re, nt, preferred_element_type=F32)
                      + lax.dot_general(keep(vim_ref[gi]), in_im, nt, preferred_element_type=F32))
                for t in range(lc):
                    for blk in range(blocks):
                        yt_ref[rows, pl.ds(lane0(b0 + i, blk, t), pack)] = (
                            yt[t * group:(t + 1) * group, blk * pack:(blk + 1) * pack])


def _ssm_conv(u, batch, log_dt, a_re, a_im, b_re, b_im, c_re, c_im):
    t, d_ssm = u.shape
    g, p = a_re.shape
    h = b_re.shape[-1]
    lc = SSM_CHUNK
    n = lc * h
    seq = t // batch
    pack = V7X_LANES
    pb = pack * lc
    n_steps = max(1, (seq // lc - 1).bit_length())
    assert seq % pb == 0 and d_ssm % V7X_LANES == 0 and batch % 2 == 0

    lb_re, lb_im, bb_re, bb_im = _ssm_discretise(log_dt, a_re, a_im, b_re, b_im)
    pw_r, pw_i = [jnp.ones_like(lb_re)], [jnp.zeros_like(lb_re)]
    for _ in range(lc):
        pw_r, pw_i = (pw_r + [pw_r[-1] * lb_re - pw_i[-1] * lb_im],
                      pw_i + [pw_r[-1] * lb_im + pw_i[-1] * lb_re])
    pw_r = jnp.stack(pw_r, axis=1)
    pw_i = jnp.stack(pw_i, axis=1)
    pair = lambda a, b: jnp.concatenate([a, b], axis=-1)
    bt_r, bt_i = bb_re.transpose(0, 2, 1), bb_im.transpose(0, 2, 1)
    tables = [pair(pw_r, pw_r), pair(pw_i, pw_i), pair(bt_r, bt_i), pair(-bt_i, bt_r),
              pair(c_re, -c_im), pair(c_re, c_re), pair(c_im, c_im)]
    sr, si = pw_r[:, lc], pw_i[:, lc]
    ar, ai = [], []
    for _ in range(n_steps):
        ar.append(pair(sr, sr))
        ai.append(pair(si, si))
        sr, si = sr * sr - si * si, 2.0 * sr * si
    ar = jnp.stack(ar, axis=1)
    ai = jnp.stack(ai, axis=1)

    per_group = lambda *shape, gs=1: pl.BlockSpec((gs,) + shape, lambda i: (i,) + (0,) * len(shape))
    gs, gm = SSM_OPS_GROUPS, SSM_MAIN_GROUPS
    assert g % gs == 0 and g % gm == 0
    krev, w, vre, vim = pl.pallas_call(
        functools.partial(_ssm_ops_kernel, lc=lc),
        grid=(g // gs,),
        in_specs=[per_group(lc + 1, 2 * p, gs=gs)] * 2 + [per_group(h, 2 * p, gs=gs)] * 5,
        out_specs=[per_group(h, n, gs=gs)] + [per_group(n, 2 * p, gs=gs)] * 3,
        out_shape=[jax.ShapeDtypeStruct((g, h, n), BF16)] + [jax.ShapeDtypeStruct((g, n, 2 * p), BF16)] * 3,
        compiler_params=_params("parallel"),
        name="ssm_ops",
    )(*tables)

    m = pl.pallas_call(
        functools.partial(_toeplitz_kernel, group=h, lc=lc),
        grid=(lc // SSM_TOEPLITZ_ROWS,),
        in_specs=[pl.BlockSpec((g, h, n), lambda i: (0, 0, 0))],
        out_specs=pl.BlockSpec((g, SSM_TOEPLITZ_ROWS, h, n), lambda i: (0, i, 0, 0)),
        out_shape=jax.ShapeDtypeStruct((g, lc, h, n), BF16),
        compiler_params=_params("parallel"),
        name="ssm_toeplitz",
    )(krev).reshape(g, n, n)

    tok_blk = pl.BlockSpec((pb, V7X_LANES), lambda i, j: (i, j))
    ch_blk = pl.BlockSpec((V7X_LANES, pb), lambda i, j: (j, i))
    relayout_grid = (t // pb, d_ssm // V7X_LANES)
    ut = pl.pallas_call(
        functools.partial(_ssm_pack_kernel, lc=lc),
        grid=relayout_grid,
        in_specs=[tok_blk],
        out_specs=ch_blk,
        out_shape=jax.ShapeDtypeStruct((d_ssm, t), BF16),
        scratch_shapes=[pltpu.VMEM((pb, V7X_LANES), F32)],
        compiler_params=_params("parallel", "parallel"),
        name="ssm_pack",
    )(u)
    yt = pl.pallas_call(
        functools.partial(_ssm_main_kernel, batch=batch, lc=lc, pack=pack, n_steps=n_steps),
        grid=(g // gm,),
        in_specs=[pl.BlockSpec((gm * h, t), lambda i: (i, 0)), per_group(n, n, gs=gm)]
        + [per_group(n, 2 * p, gs=gm)] * 3 + [per_group(n_steps, 2 * p, gs=gm)] * 2,
        out_specs=pl.BlockSpec((gm * h, t), lambda i: (i, 0)),
        out_shape=jax.ShapeDtypeStruct((d_ssm, t), F32),
        compiler_params=_params("parallel"),
        name="ssm_main",
    )(ut, m, w, vre, vim, ar, ai)
    return pl.pallas_call(
        functools.partial(_ssm_unpack_kernel, lc=lc),
        grid=relayout_grid,
        in_specs=[ch_blk],
        out_specs=tok_blk,
        out_shape=jax.ShapeDtypeStruct((t, d_ssm), F32),
        scratch_shapes=[pltpu.VMEM((pb, V7X_LANES), F32)],
        compiler_params=_params("parallel", "parallel"),
        name="ssm_unpack",
    )(yt)


def _regroup(n, dil, hop, piece=None):
    rows = n // (dil * hop)
    piece = piece or rows
    for res in range(dil * hop):
        a, r = divmod(res, dil)
        for k in range(0, rows, piece):
            yield (slice(res * rows + k, res * rows + k + piece),
                   pl.ds(r * (n // dil) + a + k * hop, piece, stride=hop))


def _attn_kernel(q_ref, k_ref, v_ref, o_ref, stage, kp4, vp4, kp16, vp16, qf, qp, ob, mb, lb, *, span, blk):
    s = pl.program_id(2)
    seq, lanes = k_ref.shape[1], k_ref.shape[2]
    dils = [d for _, d in DILATIONS]
    hop = dils[1]
    k_perm = {dils[1]: kp4, dils[2]: kp16}
    v_perm = {dils[1]: vp4, dils[2]: vp16}

    @pl.when(s == 0)
    def _():
        for src, perm in ((k_ref, k_perm), (v_ref, v_perm)):
            stage[0] = src[0].astype(F32)
            for dst, rows in _regroup(seq, dils[0], hop):
                x = stage[0, rows, :]
                stage[1, dst, :] = x
                perm[dils[1]][dst, :] = x.astype(BF16)
            for dst, rows in _regroup(seq, dils[1], hop):
                perm[dils[2]][dst, :] = stage[1, rows, :].astype(BF16)

    scale = math.log2(math.e) / math.sqrt(HEAD_DIM)
    q0 = q_ref[0].astype(F32) * scale
    qf[0] = q0
    qp[0] = q0.astype(BF16)
    for dst, rows in _regroup(span, dils[0], hop):
        x = qf[0, rows, :]
        qf[1, dst, :] = x
        qp[1, dst, :] = x.astype(BF16)
    for dst, rows in _regroup(span, dils[1], hop):
        qp[2, dst, :] = qf[1, rows, :].astype(BF16)

    head0 = lax.broadcasted_iota(jnp.int32, (blk, lanes), 1) < HEAD_DIM
    head0_kv = lax.broadcasted_iota(jnp.int32, (2 * blk, lanes), 1) < HEAD_DIM
    rel =(lax.broadcasted_iota(jnp.int32, (blk, 2 * blk), 0)
           - lax.broadcasted_iota(jnp.int32, (blk, 2 * blk), 1))

    def run_branches(first_span):
        band = None if first_span else (rel >= -blk) & (rel <= 0)
        for di, dil in enumerate(dils):
            per_res = span // dil
            n_blk = per_res // blk
            res_len = seq // dil

            def body(it, carry, di=di, dil=dil, per_res=per_res, n_blk=n_blk, res_len=res_len):
                res = it // n_blk
                jj = it % n_blk
                qb = s * per_res + jj * blk
                kb = jnp.maximum(qb - blk, 0)
                row0 = pl.multiple_of(res * per_res + jj * blk, blk)
                koff = pl.multiple_of(res * res_len + kb, blk)
                q = qp[di, pl.ds(row0, blk), :]
                if dil == 1:
                    kk = k_ref[0, pl.ds(koff, 2 * blk), :]
                    vv = v_ref[0, pl.ds(koff, 2 * blk), :]
                else:
                    kk = k_perm[dil][pl.ds(koff, 2 * blk), :]
                    vv = v_perm[dil][pl.ds(koff, 2 * blk), :]
                if first_span:
                    lo = kb - qb
                    valid = (rel >= lo) & (rel <= lo + blk)
                else:
                    valid = band
                outs, maxs = [], []
                for head_mask, kv_mask in ((head0, head0_kv), (~head0, ~head0_kv)):
                    qh = jnp.where(head_mask, q, jnp.zeros_like(q))
                    sc = lax.dot_general(qh, kk, (((1,), (1,)), ((), ())), preferred_element_type=F32)
                    sc = jnp.where(valid, sc, NEG_INF)
                    m = jnp.max(sc, axis=-1, keepdims=True)
                    p = jnp.exp2(sc - m)
                    maxs.append(m)
                    outs.append(jnp.dot(p.astype(BF16), jnp.where(kv_mask, vv, jnp.ones_like(vv)),
                                        preferred_element_type=F32))
                ob[di, pl.ds(row0, blk), :] = jnp.where(head0, outs[0], outs[1])
                mb[di, pl.ds(row0, blk), :] = jnp.where(head0, maxs[0], maxs[1])
                lb[di, pl.ds(row0, blk), :] = jnp.where(head0, outs[1], outs[0])
                return carry

            lax.fori_loop(0, dil * n_blk, body, 0, unroll=True)

    pl.when(s == 0)(functools.partial(run_branches, True))
    pl.when(s != 0)(functools.partial(run_branches, False))

    unswap = lambda x: pltpu.roll(x, HEAD_DIM, 1)
    piece = ATTN_MERGE_ROWS
    last = len(dils) - 1
    for di in range(last, 0, -1):
        for src, rows in _regroup(span, dils[di - 1], hop, piece):
            m_a = mb[di - 1, rows, :]
            m_b = mb[di, src, :]
            l_b = lb[di, src, :]
            if di == last:
                l_b = unswap(l_b)
            m_new = jnp.maximum(m_a, m_b)
            w_a = jnp.exp2(m_a - m_new)
            w_b = jnp.exp2(m_b - m_new)
            ob[di - 1, rows, :] = ob[di - 1, rows, :] * w_a + ob[di, src, :] * w_b
            lb[di - 1, rows, :] = unswap(lb[di - 1, rows, :]) * w_a + l_b * w_b
            mb[di - 1, rows, :] = m_new
    o_ref[0] = (ob[0] / lb[0]).astype(o_ref.dtype)


def _attention(qkv, batch):
    t, c = qkv.shape[0], qkv.shape[1] // 3
    seq = t // batch
    span, blk = ATTN_SPAN, ATTN_BLOCK
    dils = [d for _, d in DILATIONS]
    assert dils == [1, 4, 16] and all(w // d == blk and span % (blk * d) == 0 for w, d in DILATIONS)
    assert seq % span == 0 and V7X_LANES % HEAD_DIM == 0
    qkv3 = qkv.reshape(batch, seq, 3 * c)
    lanes = V7X_LANES
    n_hp = c // lanes
    full = lambda part: pl.BlockSpec((1, seq, lanes), lambda b, hp, s: (b, 0, part * n_hp + hp))
    tile = pl.BlockSpec((1, span, lanes), lambda b, hp, s: (b, s, hp))
    out = pl.pallas_call(
        functools.partial(_attn_kernel, span=span, blk=blk),
        grid=(batch, n_hp, seq // span),
        in_specs=[tile, full(1), full(2)],
        out_specs=tile,
        out_shape=jax.ShapeDtypeStruct((batch, seq, c), BF16),
        scratch_shapes=[pltpu.VMEM((2, seq, lanes), F32)]
        + [pltpu.VMEM((seq, lanes), BF16)] * 4
        + [pltpu.VMEM((2, span, lanes), F32), pltpu.VMEM((len(dils), span, lanes), BF16)]
        + [pltpu.VMEM((len(dils), span, lanes), F32)] * 3,
        compiler_params=_params("parallel", "parallel", "arbitrary"),
        name="attn",
    )(qkv3, qkv3, qkv3)
    return out.reshape(t, c)


def _post_kernel(yc_ref, u_ref, att_ref, h_ref, d_ref, wglu_ref, bglu_ref, gs_ref, ga_ref, wo_ref, o_ref):
    y = jax.nn.gelu(yc_ref[...] + d_ref[...] * u_ref[...])
    gate = jax.nn.sigmoid(jnp.dot(y.astype(BF16), wglu_ref[...], preferred_element_type=F32) + bglu_ref[...])
    ys = _rmsnorm(y * gate, gs_ref[...]).astype(BF16)
    ya = _rmsnorm(att_ref[...].astype(F32), ga_ref[...]).astype(BF16)
    d_ssm = ys.shape[1]
    o_ref[...] = (h_ref[...]
                  + jnp.dot(ys, wo_ref[:d_ssm, :], preferred_element_type=F32)
                  + jnp.dot(ya, wo_ref[d_ssm:, :], preferred_element_type=F32))


def _post(yc, u, att, h, d, w_glu, b_glu, g_ssm, g_att, w_out, *, tm=POST_TOKENS):
    t, dm = h.shape
    c = yc.shape[1]
    row = lambda i: (i, 0)
    const = lambda i: (0, 0)
    return pl.pallas_call(
        _post_kernel,
        grid=(t // tm,),
        in_specs=[pl.BlockSpec((tm, c), row), pl.BlockSpec((tm, c), row), pl.BlockSpec((tm, c), row),
                  pl.BlockSpec((tm, dm), row),
                  pl.BlockSpec((1, c), const), pl.BlockSpec((c, c), const), pl.BlockSpec((1, c), const),
                  pl.BlockSpec((1, c), const), pl.BlockSpec((1, c), const), pl.BlockSpec((2 * c, dm), const)],
        out_specs=pl.BlockSpec((tm, dm), row),
        out_shape=jax.ShapeDtypeStruct((t, dm), F32),
        compiler_params=_params("parallel"),
        name="post",
    )(yc, u, att, h, d.reshape(1, c), w_glu, b_glu.reshape(1, c), g_ssm.reshape(1, c), g_att.reshape(1, c), w_out)


def kernel(x, ffn1_norm, ffn1_w_gate, ffn1_w_up, ffn1_w_down, mix_norm, w_in, ssm_log_dt, ssm_a_re, ssm_a_im, ssm_b_re, ssm_b_im, ssm_c_re, ssm_c_im, ssm_d, ssm_w_glu, ssm_b_glu, ssm_out_norm, attn_out_norm, w_out, ffn2_norm, ffn2_w_gate, ffn2_w_up, ffn2_w_down, final_norm):
    batch, seq, d_model = x.shape
    depth = ffn1_norm.shape[0]
    h = x.reshape(batch * seq, d_model)
    for l in range(depth):
        last = l == depth - 1
        ffn_w = lambda *ws: [w.astype(BF16) for w in ws]
        h = _ffn(h, ffn1_norm[l], *ffn_w(ffn1_w_gate[l], ffn1_w_up[l], ffn1_w_down[l]))
        qkv, u = _inproj(h, mix_norm[l], w_in[l].astype(BF16))
        yc = _ssm_conv(u, batch, ssm_log_dt[l], ssm_a_re[l], ssm_a_im[l], ssm_b_re[l], ssm_b_im[l],
                       ssm_c_re[l], ssm_c_im[l])
        att = _attention(qkv, batch)
        h = _post(yc, u, att, h, ssm_d[l], ssm_w_glu[l].astype(BF16), ssm_b_glu[l], ssm_out_norm[l],
                  attn_out_norm[l], w_out[l].astype(BF16))
        h = _ffn(h, ffn2_norm[l], *ffn_w(ffn2_w_gate[l], ffn2_w_up[l], ffn2_w_down[l]),
                 final_g=final_norm if last else None)
    return h.reshape(batch, seq, d_model)
```

```python
import functools
import math

import jax
import jax.numpy as jnp
from jax import lax
from jax.experimental import pallas as pl
from jax.experimental.pallas import tpu as pltpu

F32 = jnp.float32
BF16 = jnp.bfloat16

EPS = 1e-6
NEG_INF = -1e30
HEAD_DIM = 64
SSM_GROUP = 16
DILATIONS = ((128, 1), (512, 4), (2048, 16))

V7X_LANES = 128
V7X_BF16_SUBLANES = 16
V7X_VMEM_BYTES = 64 * 1024 * 1024
V7X_VMEM_COMPILER_RESERVE_BYTES = 6 * 1024 * 1024
VMEM_LIMIT_BYTES = V7X_VMEM_BYTES - V7X_VMEM_COMPILER_RESERVE_BYTES

FFN_TOKENS = 1024
FFN_HIDDEN = 512
FFN_BLOCKS_PER_STEP = 2
INPROJ_TOKENS = 1024
POST_TOKENS = 512
SSM_CHUNK = 32
SSM_OPS_GROUPS = 8
SSM_MAIN_GROUPS = 4
SSM_TOEPLITZ_ROWS = 8
SSM_HOP = 4
ATTN_SPAN = 2048
ATTN_BLOCK = 128
ATTN_MERGE_ROWS = 32


def _rmsnorm(x, g):
    return x * lax.rsqrt(jnp.mean(x * x, axis=-1, keepdims=True) + EPS) * g


def _params(*semantics):
    return pltpu.CompilerParams(dimension_semantics=semantics, vmem_limit_bytes=VMEM_LIMIT_BYTES)


def _ffn_kernel(*refs, n_f, n_chunks, overlap, final_norm):
    per_step = FFN_BLOCKS_PER_STEP
    h_hbm, g_ref = refs[:2]
    w_refs = [refs[2 + 3 * k:5 + 3 * k] for k in range(per_step)]
    rest = refs[2 + 3 * per_step:]
    fg_ref = rest[0] if final_norm else None
    o_hbm, acc, n_ref, in_sem, out_sem = rest[1:] if final_norm else rest
    i, j = pl.program_id(0), pl.program_id(1)
    n_blocks, n_steps = pl.num_programs(0), pl.num_programs(1)
    tm = acc.shape[1]
    slot = i % 2

    def fetch(blk, s):
        return pltpu.make_async_copy(h_hbm.at[pl.ds(blk * tm, tm), :], acc.at[s], in_sem.at[s])

    def write_back(blk, s):
        return pltpu.make_async_copy(acc.at[s], o_hbm.at[pl.ds(blk * tm, tm), :], out_sem.at[s])

    @pl.when(j == 1)
    def _():
        @pl.when(i >= 1)
        def _():
            write_back(i - 1, 1 - slot).wait()

        @pl.when(i + 1 < n_blocks)
        def _():
            fetch(i + 1, 1 - slot).start()

    def hidden_block(wg_ref, wu_ref, wd_ref, masked, n=None):
        n = n_ref[...] if n is None else n
        gate = jnp.dot(n, wg_ref[...], preferred_element_type=F32)
        up = jnp.dot(n, wu_ref[...], preferred_element_type=F32)
        a = 0.5 * (gate * jax.nn.sigmoid(gate)) * up
        if masked:
            a = jnp.where(lax.broadcasted_iota(jnp.int32, a.shape, 1) >= overlap, a, 0.0)
        a = a.astype(BF16)
        cw = acc.shape[2] // n_chunks
        for c in range(n_chunks):
            sl = slice(c * cw, (c + 1) * cw)
            acc[slot, :, sl] += jnp.dot(a, wd_ref[:, sl], preferred_element_type=F32)

    tail = n_f - per_step * (n_f // per_step) or per_step

    @pl.when(j == 0)
    def _():
        @pl.when(i == 0)
        def _():
            fetch(0, 0).start()
        fetch(i, slot).wait()
        n = _rmsnorm(acc[slot], g_ref[...]).astype(BF16)
        n_ref[...] = n
        for k in range(per_step):
            hidden_block(*w_refs[k], masked=False, n=n)

    @pl.when((j > 0) & (j < n_steps - 1))
    def _():
        for k in range(per_step):
            hidden_block(*w_refs[k], masked=False)

    @pl.when(j == n_steps - 1)
    def _():
        for k in range(tail):
            hidden_block(*w_refs[k], masked=bool(overlap) and k == tail - 1)
        if final_norm:
            acc[slot] = _rmsnorm(acc[slot], fg_ref[...])
        write_back(i, slot).start()

        @pl.when(i == n_blocks - 1)
        def _():
            write_back(i, slot).wait()


def _ffn(h, norm_g, wg, wu, wd, final_g=None, *, tm=FFN_TOKENS, tf=FFN_HIDDEN):
    t, d = h.shape
    f = wg.shape[1]
    per_step = FFN_BLOCKS_PER_STEP
    n_f = pl.cdiv(f, tf)
    n_steps = pl.cdiv(n_f, per_step)
    overlap = n_f * tf - f
    assert f >= tf and overlap % V7X_LANES == 0 and n_steps >= 2 and t % tm == 0
    final_norm = final_g is not None

    def start(k):
        last_k = n_f - 1 - (n_f - 1 - k) % per_step
        return lambda j: pl.multiple_of(jnp.minimum(jnp.minimum(per_step * j + k, last_k) * tf, f - tf), V7X_LANES)

    in_specs = [pl.BlockSpec(memory_space=pl.ANY), pl.BlockSpec((1, d), lambda i, j: (0, 0))]
    args = [h, norm_g.reshape(1, d)]
    for k in range(per_step):
        in_specs += [
            pl.BlockSpec((pl.Element(d), pl.Element(tf)), lambda i, j, s=start(k): (0, s(j))),
            pl.BlockSpec((pl.Element(d), pl.Element(tf)), lambda i, j, s=start(k): (0, s(j))),
            pl.BlockSpec((pl.Element(tf), pl.Element(d)), lambda i, j, s=start(k): (s(j), 0)),
        ]
        args += [wg, wu, wd]
    if final_norm:
        in_specs.append(pl.BlockSpec((1, d), lambda i, j: (0, 0)))
        args.append(final_g.reshape(1, d))
    return pl.pallas_call(
        functools.partial(_ffn_kernel, n_f=n_f, n_chunks=4, overlap=overlap, final_norm=final_norm),
        grid=(t // tm, n_steps),
        in_specs=in_specs,
        out_specs=pl.BlockSpec(memory_space=pl.ANY),
        out_shape=jax.ShapeDtypeStruct((t, d), F32),
        scratch_shapes=[pltpu.VMEM((2, tm, d), F32), pltpu.VMEM((tm, d), BF16),
                        pltpu.SemaphoreType.DMA((2,)), pltpu.SemaphoreType.DMA((2,))],
        compiler_params=_params("arbitrary", "arbitrary"),
        name="ffn_final" if final_norm else "ffn",
    )(*args)


def _inproj_kernel(*refs, n_qkv, n_cast):
    h_ref, g_ref, w_ref = refs[:3]
    cast_in = refs[3:3 + n_cast]
    qkv_ref, u_ref = refs[3 + n_cast:5 + n_cast]
    cast_out = refs[5 + n_cast:5 + 2 * n_cast]
    n_ref = refs[5 + 2 * n_cast]
    j = pl.program_id(1)

    def cast_rows():
        for src, dst in zip(cast_in, cast_out):
            dst[...] = src[...].astype(dst.dtype)

    @pl.when(j == 0)
    def _():
        n = _rmsnorm(h_ref[...], g_ref[...]).astype(BF16)
        n_ref[...] = n
        qkv_ref[...] = jnp.dot(n, w_ref[...], preferred_element_type=F32).astype(qkv_ref.dtype)
        cast_rows()

    @pl.when((j > 0) & (j < n_qkv))
    def _():
        qkv_ref[...] = jnp.dot(n_ref[...], w_ref[...], preferred_element_type=F32).astype(qkv_ref.dtype)
        cast_rows()

    @pl.when(j == n_qkv)
    def _():
        u_ref[...] = jnp.dot(n_ref[...], w_ref[...], preferred_element_type=F32)
        cast_rows()


def _inproj(h, norm_g, w_in, cast=(), *, tm=INPROJ_TOKENS):
    t, d = h.shape
    n_qkv = 3
    c = w_in.shape[1] // (n_qkv + 1)
    n_steps = (t // tm) * (n_qkv + 1)
    cast_specs = []
    for w in cast:
        rows = next(r for r in range(V7X_BF16_SUBLANES, w.shape[0] + 1, V7X_BF16_SUBLANES)
                    if w.shape[0] % r == 0 and w.shape[0] // r <= n_steps)
        last = w.shape[0] // rows - 1
        cast_specs.append(pl.BlockSpec(
            (rows, w.shape[1]), lambda i, j, last=last: (jnp.minimum(i * (n_qkv + 1) + j, last), 0)))
    outs = pl.pallas_call(
        functools.partial(_inproj_kernel, n_qkv=n_qkv, n_cast=len(cast)),
        grid=(t // tm, n_qkv + 1),
        in_specs=[
            pl.BlockSpec((tm, d), lambda i, j: (i, 0)),
            pl.BlockSpec((1, d), lambda i, j: (0, 0)),
            pl.BlockSpec((d, c), lambda i, j: (0, j)),
        ] + cast_specs,
        out_specs=[pl.BlockSpec((tm, c), lambda i, j: (i, jnp.minimum(j, n_qkv - 1))),
                   pl.BlockSpec((tm, c), lambda i, j: (i, 0))] + cast_specs,
        out_shape=[jax.ShapeDtypeStruct((t, n_qkv * c), BF16), jax.ShapeDtypeStruct((t, c), F32)]
        + [jax.ShapeDtypeStruct(w.shape, BF16) for w in cast],
        scratch_shapes=[pltpu.VMEM((tm, d), BF16)],
        compiler_params=_params("arbitrary", "arbitrary"),
        name="inproj",
    )(h, norm_g.reshape(1, d), w_in, *cast)
    return outs[0], outs[1], outs[2:]


def _ssm_discretise(log_dt, a_re, a_im, b_re, b_im):
    dt = jnp.exp(log_dt)[:, None]
    mag = jnp.exp(a_re * dt)
    lb_re = mag * jnp.cos(a_im * dt)
    lb_im = mag * jnp.sin(a_im * dt)
    den = a_re * a_re + a_im * a_im
    nr = lb_re - 1.0
    ni = lb_im
    f_re = (nr * a_re + ni * a_im) / den
    f_im = (ni * a_re - nr * a_im) / den
    bb_re = f_re[..., None] * b_re - f_im[..., None] * b_im
    bb_im = f_re[..., None] * b_im + f_im[..., None] * b_re
    return lb_re, lb_im, bb_re, bb_im


def _ssm_ops_kernel(pwr, pwi, bba, bbb, cca, crr, cii, krev_ref, w_ref, vre_ref, vim_ref, *, lc):
    for gi in range(pwr.shape[0]):
        cmul = lambda k, a, b, gi=gi: pwr[gi, k:k + 1, :] * a + pwi[gi, k:k + 1, :] * b
        w = jnp.concatenate([cmul(lc - 1 - tau, bba[gi], bbb[gi]) for tau in range(lc)], axis=0)
        w_ref[gi] = w.astype(BF16)
        vre_ref[gi] = jnp.concatenate([cmul(t + 1, crr[gi], -cii[gi]) for t in range(lc)], axis=0).astype(BF16)
        vim_ref[gi] = jnp.concatenate([cmul(t + 1, -cii[gi], -crr[gi]) for t in range(lc)], axis=0).astype(BF16)
        krev_ref[gi] = lax.dot_general(cca[gi], w, (((1,), (1,)), ((), ())), precision=lax.Precision.HIGHEST,
                                       preferred_element_type=F32).astype(krev_ref.dtype)


def _toeplitz_kernel(krev_ref, m_ref, *, group, lc):
    g, _, n = krev_ref.shape
    krev = krev_ref[...].reshape(g * group, n)
    diag = lax.broadcasted_iota(jnp.int32, (n, n), 0) - lax.broadcasted_iota(jnp.int32, (n, n), 1)
    for k in range(m_ref.shape[1]):
        t = pl.program_id(0) * m_ref.shape[1] + k
        shift = (diag == group * (lc - 1 - t)).astype(BF16)
        res = jnp.dot(krev, shift, preferred_element_type=F32)
        m_ref[:, k] = res.reshape(g, group, n).astype(BF16)


def _lag_rows(tau, n, lc):
    return pl.ds((tau % SSM_HOP) * (n // SSM_HOP) + tau // SSM_HOP, n // lc, stride=lc // SSM_HOP)


def _ssm_pack_kernel(u_ref, o_ref, hop_ref, *, lc):
    n = u_ref.shape[0]
    n_chunks = n // lc
    for dst, rows in _regroup(n, 1, SSM_HOP):
        hop_ref[dst, :] = u_ref[rows, :]
    for tau in range(lc):
        x = hop_ref[_lag_rows(tau, n, lc), :]
        o_ref[:, tau * n_chunks:(tau + 1) * n_chunks] = x.T.astype(o_ref.dtype)


def _ssm_unpack_kernel(y_ref, o_ref, hop_ref, *, lc):
    n = o_ref.shape[0]
    n_chunks = n // lc
    for t in range(lc):
        hop_ref[_lag_rows(t, n, lc), :] = y_ref[:, t * n_chunks:(t + 1) * n_chunks].T
    for src, rows in _regroup(n, 1, SSM_HOP):
        o_ref[rows, :] = hop_ref[src, :]


def _ssm_main_kernel(ut_ref, m_ref, w_ref, vre_ref, vim_ref, ar_ref, ai_ref, yt_ref, *, batch, lc, pack, n_steps):
    n_groups = m_ref.shape[0]
    group = ut_ref.shape[0] // n_groups
    seq = ut_ref.shape[1] // batch
    blocks = seq // (pack * lc)
    n_chunks = blocks * pack
    lanes = w_ref.shape[2]
    half = lanes // 2
    nt = (((1,), (1,)), ((), ()))
    chunk = lax.broadcasted_iota(jnp.int32, (n_chunks, lanes), 0)
    first = lax.broadcasted_iota(jnp.int32, (n_chunks, lanes), 1) < half
    first_v = lax.broadcasted_iota(jnp.int32, vre_ref.shape[1:], 1) < half
    lane0 = lambda b, blk, k: b * seq + blk * (pack * lc) + k * pack

    for gi in range(n_groups):
        rows = slice(gi * group, (gi + 1) * group)

        def load_xt(b, rows=rows):
            return jnp.concatenate(
                [jnp.concatenate([ut_ref[rows, pl.ds(lane0(b, blk, tau), pack)] for blk in range(blocks)], axis=1)
                 for tau in range(lc)], axis=0)

        for b0 in range(0, batch, 2):
            xts = [load_xt(b0), load_xt(b0 + 1)]
            sa, sb = [lax.dot_general(xt, w_ref[gi], (((0,), (0,)), ((), ())), preferred_element_type=F32)
                      for xt in xts]
            s_re = jnp.where(first, sa, pltpu.roll(sb, half, 1))
            s_im = jnp.where(first, pltpu.roll(sa, half, 1), sb)
            for j in range(n_steps):
                k = 1 << j
                shift = lambda x: jnp.where(chunk >= k, pltpu.roll(x, k, 0), 0.0)
                sh_re, sh_im = shift(s_re), shift(s_im)
                a_re, a_im = ar_ref[gi, j:j + 1, :], ai_ref[gi, j:j + 1, :]
                s_re, s_im = s_re + a_re * sh_re - a_im * sh_im, s_im + a_re * sh_im + a_im * sh_re
            enter = lambda x: jnp.where(chunk >= 1, pltpu.roll(x, 1, 0), 0.0).astype(BF16)
            in_re, in_im = enter(s_re), enter(s_im)
            for i, xt in enumerate(xts):
                own = first_v if i == 0 else ~first_v
                keep = lambda v: jnp.where(own, v, jnp.zeros_like(v))
                yt = (jnp.dot(m_ref[gi], xt, preferred_element_type=F32)
                      + lax.dot_general(keep(vre_ref[gi]), in_re, nt, preferred_element_type=F32)
                      + lax.dot_general(keep(vim_ref[gi]), in_im, nt, preferred_element_type=F32))
                for t in range(lc):
                    for blk in range(blocks):
                        yt_ref[rows, pl.ds(lane0(b0 + i, blk, t), pack)] = (
                            yt[t * group:(t + 1) * group, blk * pack:(blk + 1) * pack])


def _ssm_conv(u, batch, log_dt, a_re, a_im, b_re, b_im, c_re, c_im):
    t, d_ssm = u.shape
    g, p = a_re.shape
    h = b_re.shape[-1]
    lc = SSM_CHUNK
    n = lc * h
    seq = t // batch
    pack = V7X_LANES
    pb = pack * lc
    n_steps = max(1, (seq // lc - 1).bit_length())
    assert seq % pb == 0 and d_ssm % V7X_LANES == 0 and batch % 2 == 0

    lb_re, lb_im, bb_re, bb_im = _ssm_discretise(log_dt, a_re, a_im, b_re, b_im)
    pw_r, pw_i = [jnp.ones_like(lb_re)], [jnp.zeros_like(lb_re)]
    for _ in range(lc):
        pw_r, pw_i = (pw_r + [pw_r[-1] * lb_re - pw_i[-1] * lb_im],
                      pw_i + [pw_r[-1] * lb_im + pw_i[-1] * lb_re])
    pw_r = jnp.stack(pw_r, axis=1)
    pw_i = jnp.stack(pw_i, axis=1)
    pair = lambda a, b: jnp.concatenate([a, b], axis=-1)
    bt_r, bt_i = bb_re.transpose(0, 2, 1), bb_im.transpose(0, 2, 1)
    tables = [pair(pw_r, pw_r), pair(pw_i, pw_i), pair(bt_r, bt_i), pair(-bt_i, bt_r),
              pair(c_re, -c_im), pair(c_re, c_re), pair(c_im, c_im)]
    sr, si = pw_r[:, lc], pw_i[:, lc]
    ar, ai = [], []
    for _ in range(n_steps):
        ar.append(pair(sr, sr))
        ai.append(pair(si, si))
        sr, si = sr * sr - si * si, 2.0 * sr * si
    ar = jnp.stack(ar, axis=1)
    ai = jnp.stack(ai, axis=1)

    per_group = lambda *shape, gs=1: pl.BlockSpec((gs,) + shape, lambda i: (i,) + (0,) * len(shape))
    gs, gm = SSM_OPS_GROUPS, SSM_MAIN_GROUPS
    assert g % gs == 0 and g % gm == 0
    krev, w, vre, vim = pl.pallas_call(
        functools.partial(_ssm_ops_kernel, lc=lc),
        grid=(g // gs,),
        in_specs=[per_group(lc + 1, 2 * p, gs=gs)] * 2 + [per_group(h, 2 * p, gs=gs)] * 5,
        out_specs=[per_group(h, n, gs=gs)] + [per_group(n, 2 * p, gs=gs)] * 3,
        out_shape=[jax.ShapeDtypeStruct((g, h, n), BF16)] + [jax.ShapeDtypeStruct((g, n, 2 * p), BF16)] * 3,
        compiler_params=_params("parallel"),
        name="ssm_ops",
    )(*tables)

    m = pl.pallas_call(
        functools.partial(_toeplitz_kernel, group=h, lc=lc),
        grid=(lc // SSM_TOEPLITZ_ROWS,),
        in_specs=[pl.BlockSpec((g, h, n), lambda i: (0, 0, 0))],
        out_specs=pl.BlockSpec((g, SSM_TOEPLITZ_ROWS, h, n), lambda i: (0, i, 0, 0)),
        out_shape=jax.ShapeDtypeStruct((g, lc, h, n), BF16),
        compiler_params=_params("parallel"),
        name="ssm_toeplitz",
    )(krev).reshape(g, n, n)

    tok_blk = pl.BlockSpec((pb, V7X_LANES), lambda i, j: (i, j))
    ch_blk = pl.BlockSpec((V7X_LANES, pb), lambda i, j: (j, i))
    relayout_grid = (t // pb, d_ssm // V7X_LANES)
    ut = pl.pallas_call(
        functools.partial(_ssm_pack_kernel, lc=lc),
        grid=relayout_grid,
        in_specs=[tok_blk],
        out_specs=ch_blk,
        out_shape=jax.ShapeDtypeStruct((d_ssm, t), BF16),
        scratch_shapes=[pltpu.VMEM((pb, V7X_LANES), F32)],
        compiler_params=_params("parallel", "parallel"),
        name="ssm_pack",
    )(u)
    yt = pl.pallas_call(
        functools.partial(_ssm_main_kernel, batch=batch, lc=lc, pack=pack, n_steps=n_steps),
        grid=(g // gm,),
        in_specs=[pl.BlockSpec((gm * h, t), lambda i: (i, 0)), per_group(n, n, gs=gm)]
        + [per_group(n, 2 * p, gs=gm)] * 3 + [per_group(n_steps, 2 * p, gs=gm)] * 2,
        out_specs=pl.BlockSpec((gm * h, t), lambda i: (i, 0)),
        out_shape=jax.ShapeDtypeStruct((d_ssm, t), F32),
        compiler_params=_params("parallel"),
        name="ssm_main",
    )(ut, m, w, vre, vim, ar, ai)
    return pl.pallas_call(
        functools.partial(_ssm_unpack_kernel, lc=lc),
        grid=relayout_grid,
        in_specs=[ch_blk],
        out_specs=tok_blk,
        out_shape=jax.ShapeDtypeStruct((t, d_ssm), F32),
        scratch_shapes=[pltpu.VMEM((pb, V7X_LANES), F32)],
        compiler_params=_params("parallel", "parallel"),
        name="ssm_unpack",
    )(yt)


def _regroup(n, dil, hop, piece=None):
    rows = n // (dil * hop)
    piece = piece or rows
    for res in range(dil * hop):
        a, r = divmod(res, dil)
        for k in range(0, rows, piece):
            yield (slice(res * rows + k, res * rows + k + piece),
                   pl.ds(r * (n // dil) + a + k * hop, piece, stride=hop))


def _attn_kernel(q_ref, k_ref, v_ref, o_ref, stage, kp4, vp4, kp16, vp16, qf, qp, ob, mb, lb, *, span, blk):
    s = pl.program_id(2)
    seq, lanes = k_ref.shape[1], k_ref.shape[2]
    dils = [d for _, d in DILATIONS]
    hop = dils[1]
    k_perm = {dils[1]: kp4, dils[2]: kp16}
    v_perm = {dils[1]: vp4, dils[2]: vp16}

    @pl.when(s == 0)
    def _():
        for src, perm in ((k_ref, k_perm), (v_ref, v_perm)):
            stage[0] = src[0].astype(F32)
            for dst, rows in _regroup(seq, dils[0], hop):
                x = stage[0, rows, :]
                stage[1, dst, :] = x
                perm[dils[1]][dst, :] = x.astype(BF16)
            for dst, rows in _regroup(seq, dils[1], hop):
                perm[dils[2]][dst, :] = stage[1, rows, :].astype(BF16)

    scale = math.log2(math.e) / math.sqrt(HEAD_DIM)
    q0 = q_ref[0].astype(F32) * scale
    qf[0] = q0
    qp[0] = q0.astype(BF16)
    for dst, rows in _regroup(span, dils[0], hop):
        x = qf[0, rows, :]
        qf[1, dst, :] = x
        qp[1, dst, :] = x.astype(BF16)
    for dst, rows in _regroup(span, dils[1], hop):
        qp[2, dst, :] = qf[1, rows, :].astype(BF16)

    head0 = lax.broadcasted_iota(jnp.int32, (blk, lanes), 1) < HEAD_DIM
    head0_kv = lax.broadcasted_iota(jnp.int32, (2 * blk, lanes), 1) < HEAD_DIM
    rel = (lax.broadcasted_iota(jnp.int32, (blk, 2 * blk), 0)
           - lax.broadcasted_iota(jnp.int32, (blk, 2 * blk), 1))

    def run_branches(first_span):
        band = None if first_span else (rel >= -blk) & (rel <= 0)
        for di, dil in enumerate(dils):
            per_res = span // dil
            n_blk = per_res // blk
            res_len = seq // dil

            def body(it, carry, di=di, dil=dil, per_res=per_res, n_blk=n_blk, res_len=res_len):
                res = it // n_blk
                jj = it % n_blk
                qb = s * per_res + jj * blk
                kb = jnp.maximum(qb - blk, 0)
                row0 = pl.multiple_of(res * per_res + jj * blk, blk)
                koff = pl.multiple_of(res * res_len + kb, blk)
                q = qp[di, pl.ds(row0, blk), :]
                if dil == 1:
                    kk = k_ref[0, pl.ds(koff, 2 * blk), :]
                    vv = v_ref[0, pl.ds(koff, 2 * blk), :]
                else:
                    kk = k_perm[dil][pl.ds(koff, 2 * blk), :]
                    vv = v_perm[dil][pl.ds(koff, 2 * blk), :]
                if first_span:
                    lo = kb - qb
                    valid = (rel >= lo) & (rel <= lo + blk)
                else:
                    valid = band
                outs, maxs = [], []
                for head_mask, kv_mask in ((head0, head0_kv), (~head0, ~head0_kv)):
                    qh = jnp.where(head_mask, q, jnp.zeros_like(q))
                    sc = lax.dot_general(qh, kk, (((1,), (1,)), ((), ())), preferred_element_type=F32)
                    sc = jnp.where(valid, sc, NEG_INF)
                    m = jnp.max(sc, axis=-1, keepdims=True)
                    p = jnp.exp2(sc - m)
                    maxs.append(m)
                    outs.append(jnp.dot(p.astype(BF16), jnp.where(kv_mask, vv, jnp.ones_like(vv)),
                                        preferred_element_type=F32))
                ob[di, pl.ds(row0, blk), :] = jnp.where(head0, outs[0], outs[1])
                mb[di, pl.ds(row0, blk), :] = jnp.where(head0, maxs[0], maxs[1])
                lb[di, pl.ds(row0, blk), :] = jnp.where(head0, outs[1], outs[0])
                return carry

            lax.fori_loop(0, dil * n_blk, body, 0, unroll=True)

    pl.when(s == 0)(functools.partial(run_branches, True))
    pl.when(s != 0)(functools.partial(run_branches, False))

    unswap = lambda x: pltpu.roll(x, HEAD_DIM, 1)
    piece = ATTN_MERGE_ROWS
    last = len(dils) - 1
    for di in range(last, 0, -1):
        for src, rows in _regroup(span, dils[di - 1], hop, piece):
            m_a = mb[di - 1, rows, :]
            m_b = mb[di, src, :]
            l_b = lb[di, src, :]
            if di == last:
                l_b = unswap(l_b)
            m_new = jnp.maximum(m_a, m_b)
            w_a = jnp.exp2(m_a - m_new)
            w_b = jnp.exp2(m_b - m_new)
            ob[di - 1, rows, :] = ob[di - 1, rows, :] * w_a + ob[di, src, :] * w_b
            lb[di - 1, rows, :] = unswap(lb[di - 1, rows, :]) * w_a + l_b * w_b
            mb[di - 1, rows, :] = m_new
    o_ref[0] = (ob[0] / lb[0]).astype(o_ref.dtype)


def _attention(qkv, batch):
    t, c = qkv.shape[0], qkv.shape[1] // 3
    seq = t // batch
    span, blk = ATTN_SPAN, ATTN_BLOCK
    dils = [d for _, d in DILATIONS]
    assert dils == [1, 4, 16] and all(w // d == blk and span % (blk * d) == 0 for w, d in DILATIONS)
    assert seq % span == 0 and V7X_LANES % HEAD_DIM == 0
    qkv3 = qkv.reshape(batch, seq, 3 * c)
    lanes = V7X_LANES
    n_hp = c // lanes
    full = lambda part: pl.BlockSpec((1, seq, lanes), lambda b, hp, s: (b, 0, part * n_hp + hp))
    tile = pl.BlockSpec((1, span, lanes), lambda b, hp, s: (b, s, hp))
    out = pl.pallas_call(
        functools.partial(_attn_kernel, span=span, blk=blk),
        grid=(batch, n_hp, seq // span),
        in_specs=[tile, full(1), full(2)],
        out_specs=tile,
        out_shape=jax.ShapeDtypeStruct((batch, seq, c), BF16),
        scratch_shapes=[pltpu.VMEM((2, seq, lanes), F32)]
        + [pltpu.VMEM((seq, lanes), BF16)] * 4
        + [pltpu.VMEM((2, span, lanes), F32), pltpu.VMEM((len(dils), span, lanes), BF16)]
        + [pltpu.VMEM((len(dils), span, lanes), F32)] * 3,
        compiler_params=_params("parallel", "parallel", "arbitrary"),
        name="attn",
    )(qkv3, qkv3, qkv3)
    return out.reshape(t, c)


def _post_kernel(yc_ref, u_ref, att_ref, h_ref, d_ref, wglu_ref, bglu_ref, gs_ref, ga_ref, wo_ref, o_ref):
    y = jax.nn.gelu(yc_ref[...] + d_ref[...] * u_ref[...])
    gate = jax.nn.sigmoid(jnp.dot(y.astype(BF16), wglu_ref[...], preferred_element_type=F32) + bglu_ref[...])
    ys = _rmsnorm(y * gate, gs_ref[...]).astype(BF16)
    ya = _rmsnorm(att_ref[...].astype(F32), ga_ref[...]).astype(BF16)
    d_ssm = ys.shape[1]
    o_ref[...] = (h_ref[...]
                  + jnp.dot(ys, wo_ref[:d_ssm, :], preferred_element_type=F32)
                  + jnp.dot(ya, wo_ref[d_ssm:, :], preferred_element_type=F32))


def _post(yc, u, att, h, d, w_glu, b_glu, g_ssm, g_att, w_out, *, tm=POST_TOKENS):
    t, dm = h.shape
    c = yc.shape[1]
    row = lambda i: (i, 0)
    const = lambda i: (0, 0)
    return pl.pallas_call(
        _post_kernel,
        grid=(t // tm,),
        in_specs=[pl.BlockSpec((tm, c), row), pl.BlockSpec((tm, c), row), pl.BlockSpec((tm, c), row),
                  pl.BlockSpec((tm, dm), row),
                  pl.BlockSpec((1, c), const), pl.BlockSpec((c, c), const), pl.BlockSpec((1, c), const),
                  pl.BlockSpec((1, c), const), pl.BlockSpec((1, c), const), pl.BlockSpec((2 * c, dm), const)],
        out_specs=pl.BlockSpec((tm, dm), row),
        out_shape=jax.ShapeDtypeStruct((t, dm), F32),
        compiler_params=_params("parallel"),
        name="post",
    )(yc, u, att, h, d.reshape(1, c), w_glu, b_glu.reshape(1, c), g_ssm.reshape(1, c), g_att.reshape(1, c), w_out)


def kernel(x, ffn1_norm, ffn1_w_gate, ffn1_w_up, ffn1_w_down, mix_norm, w_in, ssm_log_dt, ssm_a_re, ssm_a_im, ssm_b_re, ssm_b_im, ssm_c_re, ssm_c_im, ssm_d, ssm_w_glu, ssm_b_glu, ssm_out_norm, attn_out_norm, w_out, ffn2_norm, ffn2_w_gate, ffn2_w_up, ffn2_w_down, final_norm):
    batch, seq, d_model = x.shape
    depth = ffn1_norm.shape[0]
    h = x.reshape(batch * seq, d_model)
    for l in range(depth):
        last = l == depth - 1
        h = _ffn(h, ffn1_norm[l], *[w[l].astype(BF16) for w in (ffn1_w_gate, ffn1_w_up, ffn1_w_down)])
        later = (ssm_w_glu[l], w_out[l], ffn2_w_gate[l], ffn2_w_up[l], ffn2_w_down[l])
        qkv, u, (w_glu_bf, w_out_bf, *ffn2_w_bf) = _inproj(h, mix_norm[l], w_in[l].astype(BF16), later)
        yc = _ssm_conv(u, batch, ssm_log_dt[l], ssm_a_re[l], ssm_a_im[l], ssm_b_re[l], ssm_b_im[l],
                       ssm_c_re[l], ssm_c_im[l])
        att = _attention(qkv, batch)
        h = _post(yc, u, att, h, ssm_d[l], w_glu_bf, ssm_b_glu[l], ssm_out_norm[l], attn_out_norm[l], w_out_bf)
        h = _ffn(h, ffn2_norm[l], *ffn2_w_bf, final_g=final_norm if last else None)
    return h.reshape(batch, seq, d_model)
```

```python
import functools
import math

import jax
import jax.numpy as jnp
from jax import lax
from jax.experimental import pallas as pl
from jax.experimental.pallas import tpu as pltpu

F32 = jnp.float32
BF16 = jnp.bfloat16

EPS = 1e-6
NEG_INF = -1e30
HEAD_DIM = 64
SSM_GROUP = 16
DILATIONS = ((128, 1), (512, 4), (2048, 16))

V7X_LANES = 128
V7X_BF16_SUBLANES = 16
V7X_VMEM_BYTES = 64 * 1024 * 1024
V7X_VMEM_COMPILER_RESERVE_BYTES = 6 * 1024 * 1024
VMEM_LIMIT_BYTES = V7X_VMEM_BYTES - V7X_VMEM_COMPILER_RESERVE_BYTES

FFN_TOKENS = 1024
FFN_HIDDEN = 512
FFN_BLOCKS_PER_STEP = 2
INPROJ_TOKENS = 1024
POST_TOKENS = 512
SSM_CHUNK = 32
SSM_OPS_GROUPS = 8
SSM_MAIN_GROUPS = 4
SSM_TOEPLITZ_ROWS = 8
SSM_HOP = 4
ATTN_SPAN = 2048
ATTN_BLOCK = 128
ATTN_MERGE_ROWS = 32


def _rmsnorm(x, g):
    return x * lax.rsqrt(jnp.mean(x * x, axis=-1, keepdims=True) + EPS) * g


def _params(*semantics):
    return pltpu.CompilerParams(dimension_semantics=semantics, vmem_limit_bytes=VMEM_LIMIT_BYTES)


def _row_cast_specs(mats, n_steps, step):
    specs = []
    for w in mats:
        rows = next(r for r in range(V7X_BF16_SUBLANES, w.shape[0] + 1, V7X_BF16_SUBLANES)
                    if w.shape[0] % r == 0 and w.shape[0] // r <= n_steps)
        last = w.shape[0] // rows - 1
        specs.append(pl.BlockSpec((rows, w.shape[1]), lambda *ids, last=last: (jnp.minimum(step(*ids), last), 0)))
    return specs


def _cast_rows(cast_in, cast_out):
    for src, dst in zip(cast_in, cast_out):
        dst[...] = src[...].astype(dst.dtype)


def _ffn_kernel(*refs, n_f, n_chunks, overlap, final_norm, n_cast):
    per_step = FFN_BLOCKS_PER_STEP
    h_hbm, g_ref = refs[:2]
    w_refs = [refs[2 + 3 * k:5 + 3 * k] for k in range(per_step)]
    rest = refs[2 + 3 * per_step:]
    fg_ref = rest[0] if final_norm else None
    rest = rest[1:] if final_norm else rest
    cast_in, o_hbm, cast_out = rest[:n_cast], rest[n_cast], rest[n_cast + 1:2 * n_cast + 1]
    acc, n_ref, in_sem, out_sem = rest[2 * n_cast + 1:]
    cast_rows = functools.partial(_cast_rows, cast_in, cast_out)
    i, j = pl.program_id(0), pl.program_id(1)
    n_blocks, n_steps = pl.num_programs(0), pl.num_programs(1)
    tm = acc.shape[1]
    slot = i % 2

    def fetch(blk, s):
        return pltpu.make_async_copy(h_hbm.at[pl.ds(blk * tm, tm), :], acc.at[s], in_sem.at[s])

    def write_back(blk, s):
        return pltpu.make_async_copy(acc.at[s], o_hbm.at[pl.ds(blk * tm, tm), :], out_sem.at[s])

    @pl.when(j == 1)
    def _():
        @pl.when(i >= 1)
        def _():
            write_back(i - 1, 1 - slot).wait()

        @pl.when(i + 1 < n_blocks)
        def _():
            fetch(i + 1, 1 - slot).start()

    def hidden_block(wg_ref, wu_ref, wd_ref, masked, n=None):
        n = n_ref[...] if n is None else n
        gate = jnp.dot(n, wg_ref[...], preferred_element_type=F32)
        up = jnp.dot(n, wu_ref[...], preferred_element_type=F32)
        a = 0.5 * (gate * jax.nn.sigmoid(gate)) * up
        if masked:
            a = jnp.where(lax.broadcasted_iota(jnp.int32, a.shape, 1) >= overlap, a, 0.0)
        a = a.astype(BF16)
        cw = acc.shape[2] // n_chunks
        for c in range(n_chunks):
            sl = slice(c * cw, (c + 1) * cw)
            acc[slot, :, sl] += jnp.dot(a, wd_ref[:, sl], preferred_element_type=F32)

    tail = n_f - per_step * (n_f // per_step) or per_step

    @pl.when(j == 0)
    def _():
        @pl.when(i == 0)
        def _():
            fetch(0, 0).start()
        fetch(i, slot).wait()
        n = _rmsnorm(acc[slot], g_ref[...]).astype(BF16)
        n_ref[...] = n
        for k in range(per_step):
            hidden_block(*w_refs[k], masked=False, n=n)
        cast_rows()

    @pl.when((j > 0) & (j < n_steps - 1))
    def _():
        for k in range(per_step):
            hidden_block(*w_refs[k], masked=False)
        cast_rows()

    @pl.when(j == n_steps - 1)
    def _():
        for k in range(tail):
            hidden_block(*w_refs[k], masked=bool(overlap) and k == tail - 1)
        cast_rows()
        if final_norm:
            acc[slot] = _rmsnorm(acc[slot], fg_ref[...])
        write_back(i, slot).start()

        @pl.when(i == n_blocks - 1)
        def _():
            write_back(i, slot).wait()


def _ffn(h, norm_g, wg, wu, wd, final_g=None, cast=(), *, tm=FFN_TOKENS, tf=FFN_HIDDEN):
    t, d = h.shape
    f = wg.shape[1]
    per_step = FFN_BLOCKS_PER_STEP
    n_f = pl.cdiv(f, tf)
    n_steps = pl.cdiv(n_f, per_step)
    overlap = n_f * tf - f
    assert f >= tf and overlap % V7X_LANES == 0 and n_steps >= 2 and t % tm == 0
    final_norm = final_g is not None

    def start(k):
        last_k = n_f - 1 - (n_f - 1 - k) % per_step
        return lambda j: pl.multiple_of(jnp.minimum(jnp.minimum(per_step * j + k, last_k) * tf, f - tf), V7X_LANES)

    in_specs = [pl.BlockSpec(memory_space=pl.ANY), pl.BlockSpec((1, d), lambda i, j: (0, 0))]
    args = [h, norm_g.reshape(1, d)]
    for k in range(per_step):
        in_specs += [
            pl.BlockSpec((pl.Element(d), pl.Element(tf)), lambda i, j, s=start(k): (0, s(j))),
            pl.BlockSpec((pl.Element(d), pl.Element(tf)), lambda i, j, s=start(k): (0, s(j))),
            pl.BlockSpec((pl.Element(tf), pl.Element(d)), lambda i, j, s=start(k): (s(j), 0)),
        ]
        args += [wg, wu, wd]
    if final_norm:
        in_specs.append(pl.BlockSpec((1, d), lambda i, j: (0, 0)))
        args.append(final_g.reshape(1, d))
    cast_specs = _row_cast_specs(cast, (t // tm) * n_steps, lambda i, j: i * n_steps + j)
    outs = pl.pallas_call(
        functools.partial(_ffn_kernel, n_f=n_f, n_chunks=4, overlap=overlap, final_norm=final_norm,
                          n_cast=len(cast)),
        grid=(t // tm, n_steps),
        in_specs=in_specs + cast_specs,
        out_specs=[pl.BlockSpec(memory_space=pl.ANY)] + cast_specs,
        out_shape=[jax.ShapeDtypeStruct((t, d), F32)] + [jax.ShapeDtypeStruct(w.shape, BF16) for w in cast],
        scratch_shapes=[pltpu.VMEM((2, tm, d), F32), pltpu.VMEM((tm, d), BF16),
                        pltpu.SemaphoreType.DMA((2,)), pltpu.SemaphoreType.DMA((2,))],
        compiler_params=_params("arbitrary", "arbitrary"),
        name="ffn_final" if final_norm else "ffn",
    )(*args, *cast)
    return outs[0], outs[1:]


def _inproj_kernel(*refs, n_qkv, n_cast):
    h_ref, g_ref, w_ref = refs[:3]
    cast_in = refs[3:3 + n_cast]
    qkv_ref, u_ref = refs[3 + n_cast:5 + n_cast]
    cast_out = refs[5 + n_cast:5 + 2 * n_cast]
    n_ref = refs[5 + 2 * n_cast]
    cast_rows = functools.partial(_cast_rows, cast_in, cast_out)
    j = pl.program_id(1)

    @pl.when(j == 0)
    def _():
        n = _rmsnorm(h_ref[...], g_ref[...]).astype(BF16)
        n_ref[...] = n
        qkv_ref[...] = jnp.dot(n, w_ref[...], preferred_element_type=F32).astype(qkv_ref.dtype)
        cast_rows()

    @pl.when((j > 0) & (j < n_qkv))
    def _():
        qkv_ref[...] = jnp.dot(n_ref[...], w_ref[...], preferred_element_type=F32).astype(qkv_ref.dtype)
        cast_rows()

    @pl.when(j == n_qkv)
    def _():
        u_ref[...] = jnp.dot(n_ref[...], w_ref[...], preferred_element_type=F32)
        cast_rows()


def _inproj(h, norm_g, w_in, cast=(), *, tm=INPROJ_TOKENS):
    t, d = h.shape
    n_qkv = 3
    c = w_in.shape[1] // (n_qkv + 1)
    cast_specs = _row_cast_specs(cast, (t // tm) * (n_qkv + 1), lambda i, j: i * (n_qkv + 1) + j)
    outs = pl.pallas_call(
        functools.partial(_inproj_kernel, n_qkv=n_qkv, n_cast=len(cast)),
        grid=(t // tm, n_qkv + 1),
        in_specs=[
            pl.BlockSpec((tm, d), lambda i, j: (i, 0)),
            pl.BlockSpec((1, d), lambda i, j: (0, 0)),
            pl.BlockSpec((d, c), lambda i, j: (0, j)),
        ] + cast_specs,
        out_specs=[pl.BlockSpec((tm, c), lambda i, j: (i, jnp.minimum(j, n_qkv - 1))),
                   pl.BlockSpec((tm, c), lambda i, j: (i, 0))] + cast_specs,
        out_shape=[jax.ShapeDtypeStruct((t, n_qkv * c), BF16), jax.ShapeDtypeStruct((t, c), F32)]
        + [jax.ShapeDtypeStruct(w.shape, BF16) for w in cast],
        scratch_shapes=[pltpu.VMEM((tm, d), BF16)],
        compiler_params=_params("arbitrary", "arbitrary"),
        name="inproj",
    )(h, norm_g.reshape(1, d), w_in, *cast)
    return outs[0], outs[1], outs[2:]


def _ssm_discretise(log_dt, a_re, a_im, b_re, b_im):
    dt = jnp.exp(log_dt)[:, None]
    mag = jnp.exp(a_re * dt)
    lb_re = mag * jnp.cos(a_im * dt)
    lb_im = mag * jnp.sin(a_im * dt)
    den = a_re * a_re + a_im * a_im
    nr = lb_re - 1.0
    ni = lb_im
    f_re = (nr * a_re + ni * a_im) / den
    f_im = (ni * a_re - nr * a_im) / den
    bb_re = f_re[..., None] * b_re - f_im[..., None] * b_im
    bb_im = f_re[..., None] * b_im + f_im[..., None] * b_re
    return lb_re, lb_im, bb_re, bb_im


def _ssm_ops_kernel(pwr, pwi, bba, bbb, cca, crr, cii, krev_ref, w_ref, vre_ref, vim_ref, *, lc):
    for gi in range(pwr.shape[0]):
        cmul = lambda k, a, b, gi=gi: pwr[gi, k:k + 1, :] * a + pwi[gi, k:k + 1, :] * b
        w = jnp.concatenate([cmul(lc - 1 - tau, bba[gi], bbb[gi]) for tau in range(lc)], axis=0)
        w_ref[gi] = w.astype(BF16)
        vre_ref[gi] = jnp.concatenate([cmul(t + 1, crr[gi], -cii[gi]) for t in range(lc)], axis=0).astype(BF16)
        vim_ref[gi] = jnp.concatenate([cmul(t + 1, -cii[gi], -crr[gi]) for t in range(lc)], axis=0).astype(BF16)
        krev_ref[gi] = lax.dot_general(cca[gi], w, (((1,), (1,)), ((), ())), precision=lax.Precision.HIGHEST,
                                       preferred_element_type=F32).astype(krev_ref.dtype)


def _toeplitz_kernel(krev_ref, m_ref, *, group, lc):
    g, _, n = krev_ref.shape
    krev = krev_ref[...].reshape(g * group, n)
    diag = lax.broadcasted_iota(jnp.int32, (n, n), 0) - lax.broadcasted_iota(jnp.int32, (n, n), 1)
    for k in range(m_ref.shape[1]):
        t = pl.program_id(0) * m_ref.shape[1] + k
        shift = (diag == group * (lc - 1 - t)).astype(BF16)
        res = jnp.dot(krev, shift, preferred_element_type=F32)
        m_ref[:, k] = res.reshape(g, group, n).astype(BF16)


def _lag_rows(tau, n, lc):
    return pl.ds((tau % SSM_HOP) * (n // SSM_HOP) + tau // SSM_HOP, n // lc, stride=lc // SSM_HOP)


def _ssm_pack_kernel(u_ref, o_ref, hop_ref, *, lc):
    n = u_ref.shape[0]
    n_chunks = n // lc
    for dst, rows in _regroup(n, 1, SSM_HOP):
        hop_ref[dst, :] = u_ref[rows, :]
    for tau in range(lc):
        x = hop_ref[_lag_rows(tau, n, lc), :]
        o_ref[:, tau * n_chunks:(tau + 1) * n_chunks] = x.T.astype(o_ref.dtype)


def _ssm_unpack_kernel(y_ref, o_ref, hop_ref, *, lc):
    n = o_ref.shape[0]
    n_chunks = n // lc
    for t in range(lc):
        hop_ref[_lag_rows(t, n, lc), :] = y_ref[:, t * n_chunks:(t + 1) * n_chunks].T
    for src, rows in _regroup(n, 1, SSM_HOP):
        o_ref[rows, :] = hop_ref[src, :]


def _ssm_main_kernel(ut_ref, m_ref, w_ref, vre_ref, vim_ref, ar_ref, ai_ref, yt_ref, *, batch, lc, pack, n_steps):
    n_groups = m_ref.shape[0]
    group = ut_ref.shape[0] // n_groups
    seq = ut_ref.shape[1] // batch
    blocks = seq // (pack * lc)
    n_chunks = blocks * pack
    lanes = w_ref.shape[2]
    half = lanes // 2
    nt = (((1,), (1,)), ((), ()))
    chunk = lax.broadcasted_iota(jnp.int32, (n_chunks, lanes), 0)
    first = lax.broadcasted_iota(jnp.int32, (n_chunks, lanes), 1) < half
    first_v = lax.broadcasted_iota(jnp.int32, vre_ref.shape[1:], 1) < half
    lane0 = lambda b, blk, k: b * seq + blk * (pack * lc) + k * pack

    for gi in range(n_groups):
        rows = slice(gi * group, (gi + 1) * group)

        def load_xt(b, rows=rows):
            return jnp.concatenate(
                [jnp.concatenate([ut_ref[rows, pl.ds(lane0(b, blk, tau), pack)] for blk in range(blocks)], axis=1)
                 for tau in range(lc)], axis=0)

        for b0 in range(0, batch, 2):
            xts = [load_xt(b0), load_xt(b0 + 1)]
            sa, sb = [lax.dot_general(xt, w_ref[gi], (((0,), (0,)), ((), ())), preferred_element_type=F32)
                      for xt in xts]
            s_re = jnp.where(first, sa, pltpu.roll(sb, half, 1))
            s_im = jnp.where(first, pltpu.roll(sa, half, 1), sb)
            for j in range(n_steps):
                k = 1 << j
                shift = lambda x: jnp.where(chunk >= k, pltpu.roll(x, k, 0), 0.0)
                sh_re, sh_im = shift(s_re), shift(s_im)
                a_re, a_im = ar_ref[gi, j:j + 1, :], ai_ref[gi, j:j + 1, :]
                s_re, s_im = s_re + a_re * sh_re - a_im * sh_im, s_im + a_re * sh_im + a_im * sh_re
            enter = lambda x: jnp.where(chunk >= 1, pltpu.roll(x, 1, 0), 0.0).astype(BF16)
            in_re, in_im = enter(s_re), enter(s_im)
            for i, xt in enumerate(xts):
                own = first_v if i == 0 else ~first_v
                keep = lambda v: jnp.where(own, v, jnp.zeros_like(v))
                yt = (jnp.dot(m_ref[gi], xt, preferred_element_type=F32)
                      + lax.dot_general(keep(vre_ref[gi]), in_re, nt, preferred_element_type=F32)
                      + lax.dot_general(keep(vim_ref[gi]), in_im, nt, preferred_element_type=F32))
                for t in range(lc):
                    for blk in range(blocks):
                        yt_ref[rows, pl.ds(lane0(b0 + i, blk, t), pack)] = (
                            yt[t * group:(t + 1) * group, blk * pack:(blk + 1) * pack])


def _ssm_conv(u, batch, log_dt, a_re, a_im, b_re, b_im, c_re, c_im):
    t, d_ssm = u.shape
    g, p = a_re.shape
    h = b_re.shape[-1]
    lc = SSM_CHUNK
    n = lc * h
    seq = t // batch
    pack = V7X_LANES
    pb = pack * lc
    n_steps = max(1, (seq // lc - 1).bit_length())
    assert seq % pb == 0 and d_ssm % V7X_LANES == 0 and batch % 2 == 0

    lb_re, lb_im, bb_re, bb_im = _ssm_discretise(log_dt, a_re, a_im, b_re, b_im)
    pw_r, pw_i = [jnp.ones_like(lb_re)], [jnp.zeros_like(lb_re)]
    for _ in range(lc):
        pw_r, pw_i = (pw_r + [pw_r[-1] * lb_re - pw_i[-1] * lb_im],
                      pw_i + [pw_r[-1] * lb_im + pw_i[-1] * lb_re])
    pw_r = jnp.stack(pw_r, axis=1)
    pw_i = jnp.stack(pw_i, axis=1)
    pair = lambda a, b: jnp.concatenate([a, b], axis=-1)
    bt_r, bt_i = bb_re.transpose(0, 2, 1), bb_im.transpose(0, 2, 1)
    tables = [pair(pw_r, pw_r), pair(pw_i, pw_i), pair(bt_r, bt_i), pair(-bt_i, bt_r),
              pair(c_re, -c_im), pair(c_re, c_re), pair(c_im, c_im)]
    sr, si = pw_r[:, lc], pw_i[:, lc]
    ar, ai = [], []
    for _ in range(n_steps):
        ar.append(pair(sr, sr))
        ai.append(pair(si, si))
        sr, si = sr * sr - si * si, 2.0 * sr * si
    ar = jnp.stack(ar, axis=1)
    ai = jnp.stack(ai, axis=1)

    per_group = lambda *shape, gs=1: pl.BlockSpec((gs,) + shape, lambda i: (i,) + (0,) * len(shape))
    gs, gm = SSM_OPS_GROUPS, SSM_MAIN_GROUPS
    assert g % gs == 0 and g % gm == 0
    krev, w, vre, vim = pl.pallas_call(
        functools.partial(_ssm_ops_kernel, lc=lc),
        grid=(g // gs,),
        in_specs=[per_group(lc + 1, 2 * p, gs=gs)] * 2 + [per_group(h, 2 * p, gs=gs)] * 5,
        out_specs=[per_group(h, n, gs=gs)] + [per_group(n, 2 * p, gs=gs)] * 3,
        out_shape=[jax.ShapeDtypeStruct((g, h, n), BF16)] + [jax.ShapeDtypeStruct((g, n, 2 * p), BF16)] * 3,
        compiler_params=_params("parallel"),
        name="ssm_ops",
    )(*tables)

    m = pl.pallas_call(
        functools.partial(_toeplitz_kernel, group=h, lc=lc),
        grid=(lc // SSM_TOEPLITZ_ROWS,),
        in_specs=[pl.BlockSpec((g, h, n), lambda i: (0, 0, 0))],
        out_specs=pl.BlockSpec((g, SSM_TOEPLITZ_ROWS, h, n), lambda i: (0, i, 0, 0)),
        out_shape=jax.ShapeDtypeStruct((g, lc, h, n), BF16),
        compiler_params=_params("parallel"),
        name="ssm_toeplitz",
    )(krev).reshape(g, n, n)

    tok_blk = pl.BlockSpec((pb, V7X_LANES), lambda i, j: (i, j))
    ch_blk = pl.BlockSpec((V7X_LANES, pb), lambda i, j: (j, i))
    relayout_grid = (t // pb, d_ssm // V7X_LANES)
    ut = pl.pallas_call(
        functools.partial(_ssm_pack_kernel, lc=lc),
        grid=relayout_grid,
        in_specs=[tok_blk],
        out_specs=ch_blk,
        out_shape=jax.ShapeDtypeStruct((d_ssm, t), BF16),
        scratch_shapes=[pltpu.VMEM((pb, V7X_LANES), F32)],
        compiler_params=_params("parallel", "parallel"),
        name="ssm_pack",
    )(u)
    yt = pl.pallas_call(
        functools.partial(_ssm_main_kernel, batch=batch, lc=lc, pack=pack, n_steps=n_steps),
        grid=(g // gm,),
        in_specs=[pl.BlockSpec((gm * h, t), lambda i: (i, 0)), per_group(n, n, gs=gm)]
        + [per_group(n, 2 * p, gs=gm)] * 3 + [per_group(n_steps, 2 * p, gs=gm)] * 2,
        out_specs=pl.BlockSpec((gm * h, t), lambda i: (i, 0)),
        out_shape=jax.ShapeDtypeStruct((d_ssm, t), F32),
        compiler_params=_params("parallel"),
        name="ssm_main",
    )(ut, m, w, vre, vim, ar, ai)
    return pl.pallas_call(
        functools.partial(_ssm_unpack_kernel, lc=lc),
        grid=relayout_grid,
        in_specs=[ch_blk],
        out_specs=tok_blk,
        out_shape=jax.ShapeDtypeStruct((t, d_ssm), F32),
        scratch_shapes=[pltpu.VMEM((pb, V7X_LANES), F32)],
        compiler_params=_params("parallel", "parallel"),
        name="ssm_unpack",
    )(yt)


def _regroup(n, dil, hop, piece=None):
    rows = n // (dil * hop)
    piece = piece or rows
    for res in range(dil * hop):
        a, r = divmod(res, dil)
        for k in range(0, rows, piece):
            yield (slice(res * rows + k, res * rows + k + piece),
                   pl.ds(r * (n // dil) + a + k * hop, piece, stride=hop))


def _attn_kernel(q_ref, k_ref, v_ref, o_ref, stage, kp4, vp4, kp16, vp16, qf, qp, ob, mb, lb, *, span, blk):
    s = pl.program_id(2)
    seq, lanes = k_ref.shape[1], k_ref.shape[2]
    dils = [d for _, d in DILATIONS]
    hop = dils[1]
    k_perm = {dils[1]: kp4, dils[2]: kp16}
    v_perm = {dils[1]: vp4, dils[2]: vp16}

    @pl.when(s == 0)
    def _():
        for src, perm in ((k_ref, k_perm), (v_ref, v_perm)):
            stage[0] = src[0].astype(F32)
            for dst, rows in _regroup(seq, dils[0], hop):
                x = stage[0, rows, :]
                stage[1, dst, :] = x
                perm[dils[1]][dst, :] = x.astype(BF16)
            for dst, rows in _regroup(seq, dils[1], hop):
                perm[dils[2]][dst, :] = stage[1, rows, :].astype(BF16)

    scale = math.log2(math.e) / math.sqrt(HEAD_DIM)
    q0 = q_ref[0].astype(F32) * scale
    qf[0] = q0
    qp[0] = q0.astype(BF16)
    for dst, rows in _regroup(span, dils[0], hop):
        x = qf[0, rows, :]
        qf[1, dst, :] = x
        qp[1, dst, :] = x.astype(BF16)
    for dst, rows in _regroup(span, dils[1], hop):
        qp[2, dst, :] = qf[1, rows, :].astype(BF16)

    head0 = lax.broadcasted_iota(jnp.int32, (blk, lanes), 1) < HEAD_DIM
    head0_kv = lax.broadcasted_iota(jnp.int32, (2 * blk, lanes), 1) < HEAD_DIM
    rel = (lax.broadcasted_iota(jnp.int32, (blk, 2 * blk), 0)
           - lax.broadcasted_iota(jnp.int32, (blk, 2 * blk), 1))

    def run_branches(first_span):
        band = None if first_span else (rel >= -blk) & (rel <= 0)
        for di, dil in enumerate(dils):
            per_res = span // dil
            n_blk = per_res // blk
            res_len = seq // dil

            def body(it, carry, di=di, dil=dil, per_res=per_res, n_blk=n_blk, res_len=res_len):
                res = it // n_blk
                jj = it % n_blk
                qb = s * per_res + jj * blk
                kb = jnp.maximum(qb - blk, 0)
                row0 = pl.multiple_of(res * per_res + jj * blk, blk)
                koff = pl.multiple_of(res * res_len + kb, blk)
                q = qp[di, pl.ds(row0, blk), :]
                if dil == 1:
                    kk = k_ref[0, pl.ds(koff, 2 * blk), :]
                    vv = v_ref[0, pl.ds(koff, 2 * blk), :]
                else:
                    kk = k_perm[dil][pl.ds(koff, 2 * blk), :]
                    vv = v_perm[dil][pl.ds(koff, 2 * blk), :]
                if first_span:
                    lo = kb - qb
                    valid = (rel >= lo) & (rel <= lo + blk)
                else:
                    valid = band
                outs, maxs = [], []
                for head_mask, kv_mask in ((head0, head0_kv), (~head0, ~head0_kv)):
                    qh = jnp.where(head_mask, q, jnp.zeros_like(q))
                    sc = lax.dot_general(qh, kk, (((1,), (1,)), ((), ())), preferred_element_type=F32)
                    sc = jnp.where(valid, sc, NEG_INF)
                    m = jnp.max(sc, axis=-1, keepdims=True)
                    p = jnp.exp2(sc - m)
                    maxs.append(m)
                    outs.append(jnp.dot(p.astype(BF16), jnp.where(kv_mask, vv, jnp.ones_like(vv)),
                                        preferred_element_type=F32))
                ob[di, pl.ds(row0, blk), :] = jnp.where(head0, outs[0], outs[1])
                mb[di, pl.ds(row0, blk), :] = jnp.where(head0, maxs[0], maxs[1])
                lb[di, pl.ds(row0, blk), :] = jnp.where(head0, outs[1], outs[0])
                return carry

            lax.fori_loop(0, dil * n_blk, body, 0, unroll=True)

    pl.when(s == 0)(functools.partial(run_branches, True))
    pl.when(s != 0)(functools.partial(run_branches, False))

    unswap = lambda x: pltpu.roll(x, HEAD_DIM, 1)
    piece = ATTN_MERGE_ROWS
    last = len(dils) - 1
    for di in range(last, 0, -1):
        for src, rows in _regroup(span, dils[di - 1], hop, piece):
            m_a = mb[di - 1, rows, :]
            m_b = mb[di, src, :]
            l_b = lb[di, src, :]
            if di == last:
                l_b = unswap(l_b)
            m_new = jnp.maximum(m_a, m_b)
            w_a = jnp.exp2(m_a - m_new)
            w_b = jnp.exp2(m_b - m_new)
            ob[di - 1, rows, :] = ob[di - 1, rows, :] * w_a + ob[di, src, :] * w_b
            lb[di - 1, rows, :] = unswap(lb[di - 1, rows, :]) * w_a + l_b * w_b
            mb[di - 1, rows, :] = m_new
    o_ref[0] = (ob[0] / lb[0]).astype(o_ref.dtype)


def _attention(qkv, batch):
    t, c = qkv.shape[0], qkv.shape[1] // 3
    seq = t // batch
    span, blk = ATTN_SPAN, ATTN_BLOCK
    dils = [d for _, d in DILATIONS]
    assert dils == [1, 4, 16] and all(w // d == blk and span % (blk * d) == 0 for w, d in DILATIONS)
    assert seq % span == 0 and V7X_LANES % HEAD_DIM == 0
    qkv3 = qkv.reshape(batch, seq, 3 * c)
    lanes = V7X_LANES
    n_hp = c // lanes
    full = lambda part: pl.BlockSpec((1, seq, lanes), lambda b, hp, s: (b, 0, part * n_hp + hp))
    tile = pl.BlockSpec((1, span, lanes), lambda b, hp, s: (b, s, hp))
    out = pl.pallas_call(
        functools.partial(_attn_kernel, span=span, blk=blk),
        grid=(batch, n_hp, seq // span),
        in_specs=[tile, full(1), full(2)],
        out_specs=tile,
        out_shape=jax.ShapeDtypeStruct((batch, seq, c), BF16),
        scratch_shapes=[pltpu.VMEM((2, seq, lanes), F32)]
        + [pltpu.VMEM((seq, lanes), BF16)] * 4
        + [pltpu.VMEM((2, span, lanes), F32), pltpu.VMEM((len(dils), span, lanes), BF16)]
        + [pltpu.VMEM((len(dils), span, lanes), F32)] * 3,
        compiler_params=_params("parallel", "parallel", "arbitrary"),
        name="attn",
    )(qkv3, qkv3, qkv3)
    return out.reshape(t, c)


def _post_kernel(yc_ref, u_ref, att_ref, h_ref, d_ref, wglu_ref, bglu_ref, gs_ref, ga_ref, wo_ref, o_ref):
    y = jax.nn.gelu(yc_ref[...] + d_ref[...] * u_ref[...])
    gate = jax.nn.sigmoid(jnp.dot(y.astype(BF16), wglu_ref[...], preferred_element_type=F32) + bglu_ref[...])
    ys = _rmsnorm(y * gate, gs_ref[...]).astype(BF16)
    ya = _rmsnorm(att_ref[...].astype(F32), ga_ref[...]).astype(BF16)
    d_ssm = ys.shape[1]
    o_ref[...] = (h_ref[...]
                  + jnp.dot(ys, wo_ref[:d_ssm, :], preferred_element_type=F32)
                  + jnp.dot(ya, wo_ref[d_ssm:, :], preferred_element_type=F32))


def _post(yc, u, att, h, d, w_glu, b_glu, g_ssm, g_att, w_out, *, tm=POST_TOKENS):
    t, dm = h.shape
    c = yc.shape[1]
    row = lambda i: (i, 0)
    const = lambda i: (0, 0)
    return pl.pallas_call(
        _post_kernel,
        grid=(t // tm,),
        in_specs=[pl.BlockSpec((tm, c), row), pl.BlockSpec((tm, c), row), pl.BlockSpec((tm, c), row),
                  pl.BlockSpec((tm, dm), row),
                  pl.BlockSpec((1, c), const), pl.BlockSpec((c, c), const), pl.BlockSpec((1, c), const),
                  pl.BlockSpec((1, c), const), pl.BlockSpec((1, c), const), pl.BlockSpec((2 * c, dm), const)],
        out_specs=pl.BlockSpec((tm, dm), row),
        out_shape=jax.ShapeDtypeStruct((t, dm), F32),
        compiler_params=_params("parallel"),
        name="post",
    )(yc, u, att, h, d.reshape(1, c), w_glu, b_glu.reshape(1, c), g_ssm.reshape(1, c), g_att.reshape(1, c), w_out)


def kernel(x, ffn1_norm, ffn1_w_gate, ffn1_w_up, ffn1_w_down, mix_norm, w_in, ssm_log_dt, ssm_a_re, ssm_a_im, ssm_b_re, ssm_b_im, ssm_c_re, ssm_c_im, ssm_d, ssm_w_glu, ssm_b_glu, ssm_out_norm, attn_out_norm, w_out, ffn2_norm, ffn2_w_gate, ffn2_w_up, ffn2_w_down, final_norm):
    batch, seq, d_model = x.shape
    depth = ffn1_norm.shape[0]
    h = x.reshape(batch * seq, d_model)
    for l in range(depth):
        last = l == depth - 1
        h, (w_in_bf,) = _ffn(h, ffn1_norm[l], *[w[l].astype(BF16) for w in (ffn1_w_gate, ffn1_w_up, ffn1_w_down)],
                             cast=(w_in[l],))
        later = (ssm_w_glu[l], w_out[l], ffn2_w_gate[l], ffn2_w_up[l], ffn2_w_down[l])
        qkv, u, (w_glu_bf, w_out_bf, *ffn2_w_bf) = _inproj(h, mix_norm[l], w_in_bf, later)
        yc = _ssm_conv(u, batch, ssm_log_dt[l], ssm_a_re[l], ssm_a_im[l], ssm_b_re[l], ssm_b_im[l],
                       ssm_c_re[l], ssm_c_im[l])
        att = _attention(qkv, batch)
        h = _post(yc, u, att, h, ssm_d[l], w_glu_bf, ssm_b_glu[l], ssm_out_norm[l], attn_out_norm[l], w_out_bf)
        h, _ = _ffn(h, ffn2_norm[l], *ffn2_w_bf, final_g=final_norm if last else None)
    return h.reshape(batch, seq, d_model)
```

```python
import functools
import math

import jax
import jax.numpy as jnp
from jax import lax
from jax.experimental import pallas as pl
from jax.experimental.pallas import tpu as pltpu

F32 = jnp.float32
BF16 = jnp.bfloat16

EPS = 1e-6
NEG_INF = -1e30
HEAD_DIM = 64
SSM_GROUP = 16
DILATIONS = ((128, 1), (512, 4), (2048, 16))

V7X_LANES = 128
V7X_BF16_SUBLANES = 16
V7X_VMEM_BYTES = 64 * 1024 * 1024
V7X_VMEM_COMPILER_RESERVE_BYTES = 6 * 1024 * 1024
VMEM_LIMIT_BYTES = V7X_VMEM_BYTES - V7X_VMEM_COMPILER_RESERVE_BYTES

FFN_TOKENS = 1024
FFN_HIDDEN = 512
FFN_BLOCKS_PER_STEP = 2
INPROJ_TOKENS = 1024
POST_TOKENS = 512
SSM_CHUNK = 32
SSM_OPS_GROUPS = 8
SSM_MAIN_GROUPS = 4
SSM_TOEPLITZ_ROWS = 1
SSM_HOP = 4
ATTN_SPAN = 2048
ATTN_BLOCK = 128
ATTN_MERGE_ROWS = 32


def _rmsnorm(x, g):
    return x * lax.rsqrt(jnp.mean(x * x, axis=-1, keepdims=True) + EPS) * g


def _params(*semantics):
    return pltpu.CompilerParams(dimension_semantics=semantics, vmem_limit_bytes=VMEM_LIMIT_BYTES)


def _row_cast_specs(mats, n_steps, step):
    specs = []
    for w in mats:
        rows = next(r for r in range(V7X_BF16_SUBLANES, w.shape[0] + 1, V7X_BF16_SUBLANES)
                    if w.shape[0] % r == 0 and w.shape[0] // r <= n_steps)
        last = w.shape[0] // rows - 1
        specs.append(pl.BlockSpec((rows, w.shape[1]), lambda *ids, last=last: (jnp.minimum(step(*ids), last), 0)))
    return specs


def _cast_rows(cast_in, cast_out):
    for src, dst in zip(cast_in, cast_out):
        dst[...] = src[...].astype(dst.dtype)


def _ffn_kernel(*refs, n_f, n_chunks, overlap, final_norm, n_cast):
    per_step = FFN_BLOCKS_PER_STEP
    h_hbm, g_ref = refs[:2]
    w_refs = [refs[2 + 3 * k:5 + 3 * k] for k in range(per_step)]
    rest = refs[2 + 3 * per_step:]
    fg_ref = rest[0] if final_norm else None
    rest = rest[1:] if final_norm else rest
    cast_in, o_hbm, cast_out = rest[:n_cast], rest[n_cast], rest[n_cast + 1:2 * n_cast + 1]
    acc, n_ref, in_sem, out_sem = rest[2 * n_cast + 1:]
    cast_rows = functools.partial(_cast_rows, cast_in, cast_out)
    i, j = pl.program_id(0), pl.program_id(1)
    n_blocks, n_steps = pl.num_programs(0), pl.num_programs(1)
    tm = acc.shape[1]
    slot = i % 2

    def fetch(blk, s):
        return pltpu.make_async_copy(h_hbm.at[pl.ds(blk * tm, tm), :], acc.at[s], in_sem.at[s])

    def write_back(blk, s):
        return pltpu.make_async_copy(acc.at[s], o_hbm.at[pl.ds(blk * tm, tm), :], out_sem.at[s])

    @pl.when(j == 1)
    def _():
        @pl.when(i >= 1)
        def _():
            write_back(i - 1, 1 - slot).wait()

        @pl.when(i + 1 < n_blocks)
        def _():
            fetch(i + 1, 1 - slot).start()

    def hidden_block(wg_ref, wu_ref, wd_ref, masked, n=None):
        n = n_ref[...] if n is None else n
        gate = jnp.dot(n, wg_ref[...], preferred_element_type=F32)
        up = jnp.dot(n, wu_ref[...], preferred_element_type=F32)
        a = 0.5 * (gate * jax.nn.sigmoid(gate)) * up
        if masked:
            a = jnp.where(lax.broadcasted_iota(jnp.int32, a.shape, 1) >= overlap, a, 0.0)
        a = a.astype(BF16)
        cw = acc.shape[2] // n_chunks
        for c in range(n_chunks):
            sl = slice(c * cw, (c + 1) * cw)
            acc[slot, :, sl] += jnp.dot(a, wd_ref[:, sl], preferred_element_type=F32)

    tail = n_f - per_step * (n_f // per_step) or per_step

    @pl.when(j == 0)
    def _():
        @pl.when(i == 0)
        def _():
            fetch(0, 0).start()
        fetch(i, slot).wait()
        n = _rmsnorm(acc[slot], g_ref[...]).astype(BF16)
        n_ref[...] = n
        for k in range(per_step):
            hidden_block(*w_refs[k], masked=False, n=n)
        cast_rows()

    @pl.when((j > 0) & (j < n_steps - 1))
    def _():
        for k in range(per_step):
            hidden_block(*w_refs[k], masked=False)
        cast_rows()

    @pl.when(j == n_steps - 1)
    def _():
        for k in range(tail):
            hidden_block(*w_refs[k], masked=bool(overlap) and k == tail - 1)
        cast_rows()
        if final_norm:
            acc[slot] = _rmsnorm(acc[slot], fg_ref[...])
        write_back(i, slot).start()

        @pl.when(i == n_blocks - 1)
        def _():
            write_back(i, slot).wait()


def _ffn(h, norm_g, wg, wu, wd, final_g=None, cast=(), *, tm=FFN_TOKENS, tf=FFN_HIDDEN):
    t, d = h.shape
    f = wg.shape[1]
    per_step = FFN_BLOCKS_PER_STEP
    n_f = pl.cdiv(f, tf)
    n_steps = pl.cdiv(n_f, per_step)
    overlap = n_f * tf - f
    assert f >= tf and overlap % V7X_LANES == 0 and n_steps >= 2 and t % tm == 0
    final_norm = final_g is not None

    def start(k):
        last_k = n_f - 1 - (n_f - 1 - k) % per_step
        return lambda j: pl.multiple_of(jnp.minimum(jnp.minimum(per_step * j + k, last_k) * tf, f - tf), V7X_LANES)

    in_specs = [pl.BlockSpec(memory_space=pl.ANY), pl.BlockSpec((1, d), lambda i, j: (0, 0))]
    args = [h, norm_g.reshape(1, d)]
    for k in range(per_step):
        in_specs += [
            pl.BlockSpec((pl.Element(d), pl.Element(tf)), lambda i, j, s=start(k): (0, s(j))),
            pl.BlockSpec((pl.Element(d), pl.Element(tf)), lambda i, j, s=start(k): (0, s(j))),
            pl.BlockSpec((pl.Element(tf), pl.Element(d)), lambda i, j, s=start(k): (s(j), 0)),
        ]
        args += [wg, wu, wd]
    if final_norm:
        in_specs.append(pl.BlockSpec((1, d), lambda i, j: (0, 0)))
        args.append(final_g.reshape(1, d))
    cast_specs = _row_cast_specs(cast, (t // tm) * n_steps, lambda i, j: i * n_steps + j)
    outs = pl.pallas_call(
        functools.partial(_ffn_kernel, n_f=n_f, n_chunks=4, overlap=overlap, final_norm=final_norm,
                          n_cast=len(cast)),
        grid=(t // tm, n_steps),
        in_specs=in_specs + cast_specs,
        out_specs=[pl.BlockSpec(memory_space=pl.ANY)] + cast_specs,
        out_shape=[jax.ShapeDtypeStruct((t, d), F32)] + [jax.ShapeDtypeStruct(w.shape, BF16) for w in cast],
        scratch_shapes=[pltpu.VMEM((2, tm, d), F32), pltpu.VMEM((tm, d), BF16),
                        pltpu.SemaphoreType.DMA((2,)), pltpu.SemaphoreType.DMA((2,))],
        compiler_params=_params("arbitrary", "arbitrary"),
        name="ffn_final" if final_norm else "ffn",
    )(*args, *cast)
    return outs[0], outs[1:]


def _inproj_kernel(*refs, n_qkv, n_cast):
    h_ref, g_ref, w_ref = refs[:3]
    cast_in = refs[3:3 + n_cast]
    qkv_ref, u_ref = refs[3 + n_cast:5 + n_cast]
    cast_out = refs[5 + n_cast:5 + 2 * n_cast]
    n_ref = refs[5 + 2 * n_cast]
    cast_rows = functools.partial(_cast_rows, cast_in, cast_out)
    j = pl.program_id(1)

    @pl.when(j == 0)
    def _():
        n = _rmsnorm(h_ref[...], g_ref[...]).astype(BF16)
        n_ref[...] = n
        qkv_ref[...] = jnp.dot(n, w_ref[...], preferred_element_type=F32).astype(qkv_ref.dtype)
        cast_rows()

    @pl.when((j > 0) & (j < n_qkv))
    def _():
        qkv_ref[...] = jnp.dot(n_ref[...], w_ref[...], preferred_element_type=F32).astype(qkv_ref.dtype)
        cast_rows()

    @pl.when(j == n_qkv)
    def _():
        u_ref[...] = jnp.dot(n_ref[...], w_ref[...], preferred_element_type=F32)
        cast_rows()


def _inproj(h, norm_g, w_in, cast=(), *, tm=INPROJ_TOKENS):
    t, d = h.shape
    n_qkv = 3
    c = w_in.shape[1] // (n_qkv + 1)
    cast_specs = _row_cast_specs(cast, (t // tm) * (n_qkv + 1), lambda i, j: i * (n_qkv + 1) + j)
    outs = pl.pallas_call(
        functools.partial(_inproj_kernel, n_qkv=n_qkv, n_cast=len(cast)),
        grid=(t // tm, n_qkv + 1),
        in_specs=[
            pl.BlockSpec((tm, d), lambda i, j: (i, 0)),
            pl.BlockSpec((1, d), lambda i, j: (0, 0)),
            pl.BlockSpec((d, c), lambda i, j: (0, j)),
        ] + cast_specs,
        out_specs=[pl.BlockSpec((tm, c), lambda i, j: (i, jnp.minimum(j, n_qkv - 1))),
                   pl.BlockSpec((tm, c), lambda i, j: (i, 0))] + cast_specs,
        out_shape=[jax.ShapeDtypeStruct((t, n_qkv * c), BF16), jax.ShapeDtypeStruct((t, c), F32)]
        + [jax.ShapeDtypeStruct(w.shape, BF16) for w in cast],
        scratch_shapes=[pltpu.VMEM((tm, d), BF16)],
        compiler_params=_params("arbitrary", "arbitrary"),
        name="inproj",
    )(h, norm_g.reshape(1, d), w_in, *cast)
    return outs[0], outs[1], outs[2:]


def _ssm_discretise(log_dt, a_re, a_im, b_re, b_im):
    dt = jnp.exp(log_dt)[:, None]
    mag = jnp.exp(a_re * dt)
    lb_re = mag * jnp.cos(a_im * dt)
    lb_im = mag * jnp.sin(a_im * dt)
    den = a_re * a_re + a_im * a_im
    nr = lb_re - 1.0
    ni = lb_im
    f_re = (nr * a_re + ni * a_im) / den
    f_im = (ni * a_re - nr * a_im) / den
    bb_re = f_re[..., None] * b_re - f_im[..., None] * b_im
    bb_im = f_re[..., None] * b_im + f_im[..., None] * b_re
    return lb_re, lb_im, bb_re, bb_im


def _ssm_ops_kernel(*refs, lc, n_cast):
    pwr, pwi, bba, bbb, cca, crr, cii = refs[:7]
    krev_ref, w_ref, vre_ref, vim_ref = refs[7 + n_cast:11 + n_cast]
    _cast_rows(refs[7:7 + n_cast], refs[11 + n_cast:])
    _ssm_ops_body(pwr, pwi, bba, bbb, cca, crr, cii, krev_ref, w_ref, vre_ref, vim_ref, lc=lc)


def _ssm_ops_body(pwr, pwi, bba, bbb, cca, crr, cii, krev_ref, w_ref, vre_ref, vim_ref, *, lc):
    for gi in range(pwr.shape[0]):
        cmul = lambda k, a, b, gi=gi: pwr[gi, k:k + 1, :] * a + pwi[gi, k:k + 1, :] * b
        w = jnp.concatenate([cmul(lc - 1 - tau, bba[gi], bbb[gi]) for tau in range(lc)], axis=0)
        w_ref[gi] = w.astype(BF16)
        vre_ref[gi] = jnp.concatenate([cmul(t + 1, crr[gi], -cii[gi]) for t in range(lc)], axis=0).astype(BF16)
        vim_ref[gi] = jnp.concatenate([cmul(t + 1, -cii[gi], -crr[gi]) for t in range(lc)], axis=0).astype(BF16)
        krev_ref[gi] = lax.dot_general(cca[gi], w, (((1,), (1,)), ((), ())), precision=lax.Precision.HIGHEST,
                                       preferred_element_type=F32).astype(krev_ref.dtype)


def _toeplitz_kernel(*refs, group, lc, n_cast):
    krev_ref, m_ref = refs[0], refs[1 + n_cast]
    _cast_rows(refs[1:1 + n_cast], refs[2 + n_cast:])
    g, _, n = krev_ref.shape
    krev = krev_ref[...].reshape(g * group, n)
    diag = lax.broadcasted_iota(jnp.int32, (n, n), 0) - lax.broadcasted_iota(jnp.int32, (n, n), 1)
    for k in range(m_ref.shape[1]):
        t = pl.program_id(0) * m_ref.shape[1] + k
        shift = (diag == group * (lc - 1 - t)).astype(BF16)
        res = jnp.dot(krev, shift, preferred_element_type=F32)
        m_ref[:, k] = res.reshape(g, group, n).astype(BF16)


def _lag_rows(tau, n, lc):
    return pl.ds((tau % SSM_HOP) * (n // SSM_HOP) + tau // SSM_HOP, n // lc, stride=lc // SSM_HOP)


def _ssm_pack_kernel(u_ref, o_ref, hop_ref, *, lc):
    n = u_ref.shape[0]
    n_chunks = n // lc
    for dst, rows in _regroup(n, 1, SSM_HOP):
        hop_ref[dst, :] = u_ref[rows, :]
    for tau in range(lc):
        x = hop_ref[_lag_rows(tau, n, lc), :]
        o_ref[:, tau * n_chunks:(tau + 1) * n_chunks] = x.T.astype(o_ref.dtype)


def _ssm_unpack_kernel(y_ref, o_ref, hop_ref, *, lc):
    n = o_ref.shape[0]
    n_chunks = n // lc
    for t in range(lc):
        hop_ref[_lag_rows(t, n, lc), :] = y_ref[:, t * n_chunks:(t + 1) * n_chunks].T
    for src, rows in _regroup(n, 1, SSM_HOP):
        o_ref[rows, :] = hop_ref[src, :]


def _ssm_main_kernel(ut_ref, m_ref, w_ref, vre_ref, vim_ref, ar_ref, ai_ref, yt_ref, *, batch, lc, pack, n_steps):
    n_groups = m_ref.shape[0]
    group = ut_ref.shape[0] // n_groups
    seq = ut_ref.shape[1] // batch
    blocks = seq // (pack * lc)
    n_chunks = blocks * pack
    lanes = w_ref.shape[2]
    half = lanes // 2
    nt = (((1,), (1,)), ((), ()))
    chunk = lax.broadcasted_iota(jnp.int32, (n_chunks, lanes), 0)
    first = lax.broadcasted_iota(jnp.int32, (n_chunks, lanes), 1) < half
    first_v = lax.broadcasted_iota(jnp.int32, vre_ref.shape[1:], 1) < half
    lane0 = lambda b, blk, k: b * seq + blk * (pack * lc) + k * pack

    for gi in range(n_groups):
        rows = slice(gi * group, (gi + 1) * group)

        def load_xt(b, rows=rows):
            return jnp.concatenate(
                [jnp.concatenate([ut_ref[rows, pl.ds(lane0(b, blk, tau), pack)] for blk in range(blocks)], axis=1)
                 for tau in range(lc)], axis=0)

        for b0 in range(0, batch, 2):
            xts = [load_xt(b0), load_xt(b0 + 1)]
            sa, sb = [lax.dot_general(xt, w_ref[gi], (((0,), (0,)), ((), ())), preferred_element_type=F32)
                      for xt in xts]
            s_re = jnp.where(first, sa, pltpu.roll(sb, half, 1))
            s_im = jnp.where(first, pltpu.roll(sa, half, 1), sb)
            for j in range(n_steps):
                k = 1 << j
                shift = lambda x: jnp.where(chunk >= k, pltpu.roll(x, k, 0), 0.0)
                sh_re, sh_im = shift(s_re), shift(s_im)
                a_re, a_im = ar_ref[gi, j:j + 1, :], ai_ref[gi, j:j + 1, :]
                s_re, s_im = s_re + a_re * sh_re - a_im * sh_im, s_im + a_re * sh_im + a_im * sh_re
            enter = lambda x: jnp.where(chunk >= 1, pltpu.roll(x, 1, 0), 0.0).astype(BF16)
            in_re, in_im = enter(s_re), enter(s_im)
            for i, xt in enumerate(xts):
                own = first_v if i == 0 else ~first_v
                keep = lambda v: jnp.where(own, v, jnp.zeros_like(v))
                yt = (jnp.dot(m_ref[gi], xt, preferred_element_type=F32)
                      + lax.dot_general(keep(vre_ref[gi]), in_re, nt, preferred_element_type=F32)
                      + lax.dot_general(keep(vim_ref[gi]), in_im, nt, preferred_element_type=F32))
                for t in range(lc):
                    for blk in range(blocks):
                        yt_ref[rows, pl.ds(lane0(b0 + i, blk, t), pack)] = (
                            yt[t * group:(t + 1) * group, blk * pack:(blk + 1) * pack])


def _ssm_operators(seq, log_dt, a_re, a_im, b_re, b_im, c_re, c_im, cast_ops=(), cast_toeplitz=()):
    g, p = a_re.shape
    h = b_re.shape[-1]
    lc = SSM_CHUNK
    n = lc * h
    n_steps = max(1, (seq // lc - 1).bit_length())

    lb_re, lb_im, bb_re, bb_im = _ssm_discretise(log_dt, a_re, a_im, b_re, b_im)
    pw_r, pw_i = [jnp.ones_like(lb_re)], [jnp.zeros_like(lb_re)]
    for _ in range(lc):
        pw_r, pw_i = (pw_r + [pw_r[-1] * lb_re - pw_i[-1] * lb_im],
                      pw_i + [pw_r[-1] * lb_im + pw_i[-1] * lb_re])
    pw_r = jnp.stack(pw_r, axis=1)
    pw_i = jnp.stack(pw_i, axis=1)
    pair = lambda a, b: jnp.concatenate([a, b], axis=-1)
    bt_r, bt_i = bb_re.transpose(0, 2, 1), bb_im.transpose(0, 2, 1)
    tables = [pair(pw_r, pw_r), pair(pw_i, pw_i), pair(bt_r, bt_i), pair(-bt_i, bt_r),
              pair(c_re, -c_im), pair(c_re, c_re), pair(c_im, c_im)]
    sr, si = pw_r[:, lc], pw_i[:, lc]
    ar, ai = [], []
    for _ in range(n_steps):
        ar.append(pair(sr, sr))
        ai.append(pair(si, si))
        sr, si = sr * sr - si * si, 2.0 * sr * si
    ar = jnp.stack(ar, axis=1)
    ai = jnp.stack(ai, axis=1)

    gs = SSM_OPS_GROUPS
    assert g % gs == 0
    per_group = lambda *shape: pl.BlockSpec((gs,) + shape, lambda i: (i,) + (0,) * len(shape))
    cast_specs = _row_cast_specs(cast_ops, g // gs, lambda i: i)
    krev, w, vre, vim, *ops_casts = pl.pallas_call(
        functools.partial(_ssm_ops_kernel, lc=lc, n_cast=len(cast_ops)),
        grid=(g // gs,),
        in_specs=[per_group(lc + 1, 2 * p)] * 2 + [per_group(h, 2 * p)] * 5 + cast_specs,
        out_specs=[per_group(h, n)] + [per_group(n, 2 * p)] * 3 + cast_specs,
        out_shape=[jax.ShapeDtypeStruct((g, h, n), BF16)] + [jax.ShapeDtypeStruct((g, n, 2 * p), BF16)] * 3
        + [jax.ShapeDtypeStruct(x.shape, BF16) for x in cast_ops],
        compiler_params=_params("arbitrary"),
        name="ssm_ops",
    )(*tables, *cast_ops)

    n_rows = lc // SSM_TOEPLITZ_ROWS
    cast_specs = _row_cast_specs(cast_toeplitz, n_rows, lambda i: i)
    m, *toeplitz_casts = pl.pallas_call(
        functools.partial(_toeplitz_kernel, group=h, lc=lc, n_cast=len(cast_toeplitz)),
        grid=(n_rows,),
        in_specs=[pl.BlockSpec((g, h, n), lambda i: (0, 0, 0))] + cast_specs,
        out_specs=[pl.BlockSpec((g, SSM_TOEPLITZ_ROWS, h, n), lambda i: (0, i, 0, 0))] + cast_specs,
        out_shape=[jax.ShapeDtypeStruct((g, lc, h, n), BF16)]
        + [jax.ShapeDtypeStruct(x.shape, BF16) for x in cast_toeplitz],
        compiler_params=_params("arbitrary"),
        name="ssm_toeplitz",
    )(krev, *cast_toeplitz)
    return (m.reshape(g, n, n), w, vre, vim, ar, ai), ops_casts, toeplitz_casts


def _ssm_conv(u, batch, operators):
    m, w, vre, vim, ar, ai = operators
    t, d_ssm = u.shape
    g, n = m.shape[:2]
    p, n_steps = w.shape[2] // 2, ar.shape[1]
    lc = SSM_CHUNK
    h = n // lc
    seq = t // batch
    pack = V7X_LANES
    pb = pack * lc
    gm = SSM_MAIN_GROUPS
    assert seq % pb == 0 and d_ssm % V7X_LANES == 0 and batch % 2 == 0 and g % gm == 0
    per_group = lambda *shape, gs=1: pl.BlockSpec((gs,) + shape, lambda i: (i,) + (0,) * len(shape))

    tok_blk = pl.BlockSpec((pb, V7X_LANES), lambda i, j: (i, j))
    ch_blk = pl.BlockSpec((V7X_LANES, pb), lambda i, j: (j, i))
    relayout_grid = (t // pb, d_ssm // V7X_LANES)
    ut = pl.pallas_call(
        functools.partial(_ssm_pack_kernel, lc=lc),
        grid=relayout_grid,
        in_specs=[tok_blk],
        out_specs=ch_blk,
        out_shape=jax.ShapeDtypeStruct((d_ssm, t), BF16),
        scratch_shapes=[pltpu.VMEM((pb, V7X_LANES), F32)],
        compiler_params=_params("parallel", "parallel"),
        name="ssm_pack",
    )(u)
    yt = pl.pallas_call(
        functools.partial(_ssm_main_kernel, batch=batch, lc=lc, pack=pack, n_steps=n_steps),
        grid=(g // gm,),
        in_specs=[pl.BlockSpec((gm * h, t), lambda i: (i, 0)), per_group(n, n, gs=gm)]
        + [per_group(n, 2 * p, gs=gm)] * 3 + [per_group(n_steps, 2 * p, gs=gm)] * 2,
        out_specs=pl.BlockSpec((gm * h, t), lambda i: (i, 0)),
        out_shape=jax.ShapeDtypeStruct((d_ssm, t), F32),
        compiler_params=_params("parallel"),
        name="ssm_main",
    )(ut, m, w, vre, vim, ar, ai)
    return pl.pallas_call(
        functools.partial(_ssm_unpack_kernel, lc=lc),
        grid=relayout_grid,
        in_specs=[ch_blk],
        out_specs=tok_blk,
        out_shape=jax.ShapeDtypeStruct((t, d_ssm), F32),
        scratch_shapes=[pltpu.VMEM((pb, V7X_LANES), F32)],
        compiler_params=_params("parallel", "parallel"),
        name="ssm_unpack",
    )(yt)


def _regroup(n, dil, hop, piece=None):
    rows = n // (dil * hop)
    piece = piece or rows
    for res in range(dil * hop):
        a, r = divmod(res, dil)
        for k in range(0, rows, piece):
            yield (slice(res * rows + k, res * rows + k + piece),
                   pl.ds(r * (n // dil) + a + k * hop, piece, stride=hop))


def _attn_kernel(q_ref, k_ref, v_ref, o_ref, stage, kp4, vp4, kp16, vp16, qf, qp, ob, mb, lb, *, span, blk):
    s = pl.program_id(2)
    seq, lanes = k_ref.shape[1], k_ref.shape[2]
    dils = [d for _, d in DILATIONS]
    hop = dils[1]
    k_perm = {dils[1]: kp4, dils[2]: kp16}
    v_perm = {dils[1]: vp4, dils[2]: vp16}

    @pl.when(s == 0)
    def _():
        for src, perm in ((k_ref, k_perm), (v_ref, v_perm)):
            stage[0] = src[0].astype(F32)
            for dst, rows in _regroup(seq, dils[0], hop):
                x = stage[0, rows, :]
                stage[1, dst, :] = x
                perm[dils[1]][dst, :] = x.astype(BF16)
            for dst, rows in _regroup(seq, dils[1], hop):
                perm[dils[2]][dst, :] = stage[1, rows, :].astype(BF16)

    scale = math.log2(math.e) / math.sqrt(HEAD_DIM)
    q0 = q_ref[0].astype(F32) * scale
    qf[0] = q0
    qp[0] = q0.astype(BF16)
    for dst, rows in _regroup(span, dils[0], hop):
        x = qf[0, rows, :]
        qf[1, dst, :] = x
        qp[1, dst, :] = x.astype(BF16)
    for dst, rows in _regroup(span, dils[1], hop):
        qp[2, dst, :] = qf[1, rows, :].astype(BF16)

    head0 = lax.broadcasted_iota(jnp.int32, (blk, lanes), 1) < HEAD_DIM
    head0_kv = lax.broadcasted_iota(jnp.int32, (2 * blk, lanes), 1) < HEAD_DIM
    rel = (lax.broadcasted_iota(jnp.int32, (blk, 2 * blk), 0)
           - lax.broadcasted_iota(jnp.int32, (blk, 2 * blk), 1))

    def run_branches(first_span):
        band = None if first_span else (rel >= -blk) & (rel <= 0)
        for di, dil in enumerate(dils):
            per_res = span // dil
            n_blk = per_res // blk
            res_len = seq // dil

            def body(it, carry, di=di, dil=dil, per_res=per_res, n_blk=n_blk, res_len=res_len):
                res = it // n_blk
                jj = it % n_blk
                qb = s * per_res + jj * blk
                kb = jnp.maximum(qb - blk, 0)
                row0 = pl.multiple_of(res * per_res + jj * blk, blk)
                koff = pl.multiple_of(res * res_len + kb, blk)
                q = qp[di, pl.ds(row0, blk), :]
                if dil == 1:
                    kk = k_ref[0, pl.ds(koff, 2 * blk), :]
                    vv = v_ref[0, pl.ds(koff, 2 * blk), :]
                else:
                    kk = k_perm[dil][pl.ds(koff, 2 * blk), :]
                    vv = v_perm[dil][pl.ds(koff, 2 * blk), :]
                if first_span:
                    lo = kb - qb
                    valid = (rel >= lo) & (rel <= lo + blk)
                else:
                    valid = band
                outs, maxs = [], []
                for head_mask, kv_mask in ((head0, head0_kv), (~head0, ~head0_kv)):
                    qh = jnp.where(head_mask, q, jnp.zeros_like(q))
                    sc = lax.dot_general(qh, kk, (((1,), (1,)), ((), ())), preferred_element_type=F32)
                    sc = jnp.where(valid, sc, NEG_INF)
                    m = jnp.max(sc, axis=-1, keepdims=True)
                    p = jnp.exp2(sc - m)
                    maxs.append(m)
                    outs.append(jnp.dot(p.astype(BF16), jnp.where(kv_mask, vv, jnp.ones_like(vv)),
                                        preferred_element_type=F32))
                ob[di, pl.ds(row0, blk), :] = jnp.where(head0, outs[0], outs[1])
                mb[di, pl.ds(row0, blk), :] = jnp.where(head0, maxs[0], maxs[1])
                lb[di, pl.ds(row0, blk), :] = jnp.where(head0, outs[1], outs[0])
                return carry

            lax.fori_loop(0, dil * n_blk, body, 0, unroll=True)

    pl.when(s == 0)(functools.partial(run_branches, True))
    pl.when(s != 0)(functools.partial(run_branches, False))

    unswap = lambda x: pltpu.roll(x, HEAD_DIM, 1)
    piece = ATTN_MERGE_ROWS
    last = len(dils) - 1
    for di in range(last, 0, -1):
        for src, rows in _regroup(span, dils[di - 1], hop, piece):
            m_a = mb[di - 1, rows, :]
            m_b = mb[di, src, :]
            l_b = lb[di, src, :]
            if di == last:
                l_b = unswap(l_b)
            m_new = jnp.maximum(m_a, m_b)
            w_a = jnp.exp2(m_a - m_new)
            w_b = jnp.exp2(m_b - m_new)
            ob[di - 1, rows, :] = ob[di - 1, rows, :] * w_a + ob[di, src, :] * w_b
            lb[di - 1, rows, :] = unswap(lb[di - 1, rows, :]) * w_a + l_b * w_b
            mb[di - 1, rows, :] = m_new
    o_ref[0] = (ob[0] / lb[0]).astype(o_ref.dtype)


def _attention(qkv, batch):
    t, c = qkv.shape[0], qkv.shape[1] // 3
    seq = t // batch
    span, blk = ATTN_SPAN, ATTN_BLOCK
    dils = [d for _, d in DILATIONS]
    assert dils == [1, 4, 16] and all(w // d == blk and span % (blk * d) == 0 for w, d in DILATIONS)
    assert seq % span == 0 and V7X_LANES % HEAD_DIM == 0
    qkv3 = qkv.reshape(batch, seq, 3 * c)
    lanes = V7X_LANES
    n_hp = c // lanes
    full = lambda part: pl.BlockSpec((1, seq, lanes), lambda b, hp, s: (b, 0, part * n_hp + hp))
    tile = pl.BlockSpec((1, span, lanes), lambda b, hp, s: (b, s, hp))
    out = pl.pallas_call(
        functools.partial(_attn_kernel, span=span, blk=blk),
        grid=(batch, n_hp, seq // span),
        in_specs=[tile, full(1), full(2)],
        out_specs=tile,
        out_shape=jax.ShapeDtypeStruct((batch, seq, c), BF16),
        scratch_shapes=[pltpu.VMEM((2, seq, lanes), F32)]
        + [pltpu.VMEM((seq, lanes), BF16)] * 4
        + [pltpu.VMEM((2, span, lanes), F32), pltpu.VMEM((len(dils), span, lanes), BF16)]
        + [pltpu.VMEM((len(dils), span, lanes), F32)] * 3,
        compiler_params=_params("parallel", "parallel", "arbitrary"),
        name="attn",
    )(qkv3, qkv3, qkv3)
    return out.reshape(t, c)


def _post_kernel(yc_ref, u_ref, att_ref, h_ref, d_ref, wglu_ref, bglu_ref, gs_ref, ga_ref, wo_ref, o_ref):
    y = jax.nn.gelu(yc_ref[...] + d_ref[...] * u_ref[...])
    gate = jax.nn.sigmoid(jnp.dot(y.astype(BF16), wglu_ref[...], preferred_element_type=F32) + bglu_ref[...])
    ys = _rmsnorm(y * gate, gs_ref[...]).astype(BF16)
    ya = _rmsnorm(att_ref[...].astype(F32), ga_ref[...]).astype(BF16)
    d_ssm = ys.shape[1]
    o_ref[...] = (h_ref[...]
                  + jnp.dot(ys, wo_ref[:d_ssm, :], preferred_element_type=F32)
                  + jnp.dot(ya, wo_ref[d_ssm:, :], preferred_element_type=F32))


def _post(yc, u, att, h, d, w_glu, b_glu, g_ssm, g_att, w_out, *, tm=POST_TOKENS):
    t, dm = h.shape
    c = yc.shape[1]
    row = lambda i: (i, 0)
    const = lambda i: (0, 0)
    return pl.pallas_call(
        _post_kernel,
        grid=(t // tm,),
        in_specs=[pl.BlockSpec((tm, c), row), pl.BlockSpec((tm, c), row), pl.BlockSpec((tm, c), row),
                  pl.BlockSpec((tm, dm), row),
                  pl.BlockSpec((1, c), const), pl.BlockSpec((c, c), const), pl.BlockSpec((1, c), const),
                  pl.BlockSpec((1, c), const), pl.BlockSpec((1, c), const), pl.BlockSpec((2 * c, dm), const)],
        out_specs=pl.BlockSpec((tm, dm), row),
        out_shape=jax.ShapeDtypeStruct((t, dm), F32),
        compiler_params=_params("parallel"),
        name="post",
    )(yc, u, att, h, d.reshape(1, c), w_glu, b_glu.reshape(1, c), g_ssm.reshape(1, c), g_att.reshape(1, c), w_out)


def kernel(x, ffn1_norm, ffn1_w_gate, ffn1_w_up, ffn1_w_down, mix_norm, w_in, ssm_log_dt, ssm_a_re, ssm_a_im, ssm_b_re, ssm_b_im, ssm_c_re, ssm_c_im, ssm_d, ssm_w_glu, ssm_b_glu, ssm_out_norm, attn_out_norm, w_out, ffn2_norm, ffn2_w_gate, ffn2_w_up, ffn2_w_down, final_norm):
    batch, seq, d_model = x.shape
    depth = ffn1_norm.shape[0]
    h = x.reshape(batch * seq, d_model)
    for l in range(depth):
        last = l == depth - 1
        ssm_operators, (w_down_bf,), (w_gate_bf, w_up_bf) = _ssm_operators(
            seq, ssm_log_dt[l], ssm_a_re[l], ssm_a_im[l], ssm_b_re[l], ssm_b_im[l], ssm_c_re[l], ssm_c_im[l],
            cast_ops=(ffn1_w_down[l],), cast_toeplitz=(ffn1_w_gate[l], ffn1_w_up[l]))
        h, (w_in_bf,) = _ffn(h, ffn1_norm[l], w_gate_bf, w_up_bf, w_down_bf, cast=(w_in[l],))
        later = (ssm_w_glu[l], w_out[l], ffn2_w_gate[l], ffn2_w_up[l], ffn2_w_down[l])
        qkv, u, (w_glu_bf, w_out_bf, *ffn2_w_bf) = _inproj(h, mix_norm[l], w_in_bf, later)
        yc = _ssm_conv(u, batch, ssm_operators)
        att = _attention(qkv, batch)
        h = _post(yc, u, att, h, ssm_d[l], w_glu_bf, ssm_b_glu[l], ssm_out_norm[l], attn_out_norm[l], w_out_bf)
        h, _ = _ffn(h, ffn2_norm[l], *ffn2_w_bf, final_g=final_norm if last else None)
    return h.reshape(batch, seq, d_model)
```

```python
import functools
import math

import jax
import jax.numpy as jnp
from jax import lax
from jax.experimental import pallas as pl
from jax.experimental.pallas import tpu as pltpu

F32 = jnp.float32
BF16 = jnp.bfloat16

EPS = 1e-6
NEG_INF = -1e30
HEAD_DIM = 64
SSM_GROUP = 16
DILATIONS = ((128, 1), (512, 4), (2048, 16))

V7X_LANES = 128
V7X_BF16_SUBLANES = 16
V7X_VMEM_BYTES = 64 * 1024 * 1024
V7X_VMEM_COMPILER_RESERVE_BYTES = 6 * 1024 * 1024
VMEM_LIMIT_BYTES = V7X_VMEM_BYTES - V7X_VMEM_COMPILER_RESERVE_BYTES

FFN_TOKENS = 1024
FFN_HIDDEN = 512
FFN_BLOCKS_PER_STEP = 2
INPROJ_TOKENS = 1024
POST_TOKENS = 512
SSM_CHUNK = 32
SSM_OPS_GROUPS = 8
SSM_MAIN_GROUPS = 4
SSM_TOEPLITZ_ROWS = 1
SSM_HOP = 4
ATTN_SPAN = 2048
ATTN_BLOCK = 128
ATTN_MERGE_ROWS = 32


def _rmsnorm(x, g):
    return x * lax.rsqrt(jnp.mean(x * x, axis=-1, keepdims=True) + EPS) * g


def _params(*semantics):
    return pltpu.CompilerParams(dimension_semantics=semantics, vmem_limit_bytes=VMEM_LIMIT_BYTES)


def _row_cast_specs(mats, n_steps, step):
    specs = []
    for w in mats:
        rows = next(r for r in range(V7X_BF16_SUBLANES, w.shape[0] + 1, V7X_BF16_SUBLANES)
                    if w.shape[0] % r == 0 and w.shape[0] // r <= n_steps)
        last = w.shape[0] // rows - 1
        specs.append(pl.BlockSpec((rows, w.shape[1]), lambda *ids, last=last: (jnp.minimum(step(*ids), last), 0)))
    return specs


def _cast_rows(cast_in, cast_out):
    for src, dst in zip(cast_in, cast_out):
        dst[...] = src[...].astype(dst.dtype)


def _ffn_kernel(*refs, n_f, n_chunks, overlap, final_norm, n_cast):
    per_step = FFN_BLOCKS_PER_STEP
    h_hbm, g_ref = refs[:2]
    w_refs = [refs[2 + 3 * k:5 + 3 * k] for k in range(per_step)]
    rest = refs[2 + 3 * per_step:]
    fg_ref = rest[0] if final_norm else None
    rest = rest[1:] if final_norm else rest
    cast_in, o_hbm, cast_out = rest[:n_cast], rest[n_cast], rest[n_cast + 1:2 * n_cast + 1]
    acc, n_ref, in_sem, out_sem = rest[2 * n_cast + 1:]
    cast_rows = functools.partial(_cast_rows, cast_in, cast_out)
    i, j = pl.program_id(0), pl.program_id(1)
    n_blocks, n_steps = pl.num_programs(0), pl.num_programs(1)
    tm = acc.shape[1]
    slot = i % 2

    def fetch(blk, s):
        return pltpu.make_async_copy(h_hbm.at[pl.ds(blk * tm, tm), :], acc.at[s], in_sem.at[s])

    def write_back(blk, s):
        return pltpu.make_async_copy(acc.at[s], o_hbm.at[pl.ds(blk * tm, tm), :], out_sem.at[s])

    @pl.when(j == 1)
    def _():
        @pl.when(i >= 1)
        def _():
            write_back(i - 1, 1 - slot).wait()

        @pl.when(i + 1 < n_blocks)
        def _():
            fetch(i + 1, 1 - slot).start()

    def hidden_block(wg_ref, wu_ref, wd_ref, masked, n=None):
        n = n_ref[...] if n is None else n
        gate = jnp.dot(n, wg_ref[...], preferred_element_type=F32)
        up = jnp.dot(n, wu_ref[...], preferred_element_type=F32)
        a = 0.5 * (gate * jax.nn.sigmoid(gate)) * up
        if masked:
            a = jnp.where(lax.broadcasted_iota(jnp.int32, a.shape, 1) >= overlap, a, 0.0)
        a = a.astype(BF16)
        cw = acc.shape[2] // n_chunks
        for c in range(n_chunks):
            sl = slice(c * cw, (c + 1) * cw)
            acc[slot, :, sl] += jnp.dot(a, wd_ref[:, sl], preferred_element_type=F32)

    tail = n_f - per_step * (n_f // per_step) or per_step

    @pl.when(j == 0)
    def _():
        @pl.when(i == 0)
        def _():
            fetch(0, 0).start()
        fetch(i, slot).wait()
        n = _rmsnorm(acc[slot], g_ref[...]).astype(BF16)
        n_ref[...] = n
        for k in range(per_step):
            hidden_block(*w_refs[k], masked=False, n=n)
        cast_rows()

    @pl.when((j > 0) & (j < n_steps - 1))
    def _():
        for k in range(per_step):
            hidden_block(*w_refs[k], masked=False)
        cast_rows()

    @pl.when(j == n_steps - 1)
    def _():
        for k in range(tail):
            hidden_block(*w_refs[k], masked=bool(overlap) and k == tail - 1)
        cast_rows()
        if final_norm:
            acc[slot] = _rmsnorm(acc[slot], fg_ref[...])
        write_back(i, slot).start()

        @pl.when(i == n_blocks - 1)
        def _():
            write_back(i, slot).wait()


def _ffn(h, norm_g, wg, wu, wd, final_g=None, cast=(), *, tm=FFN_TOKENS, tf=FFN_HIDDEN):
    t, d = h.shape
    f = wg.shape[1]
    per_step = FFN_BLOCKS_PER_STEP
    n_f = pl.cdiv(f, tf)
    n_steps = pl.cdiv(n_f, per_step)
    overlap = n_f * tf - f
    assert f >= tf and overlap % V7X_LANES == 0 and n_steps >= 2 and t % tm == 0
    final_norm = final_g is not None

    def start(k):
        last_k = n_f - 1 - (n_f - 1 - k) % per_step
        return lambda j: pl.multiple_of(jnp.minimum(jnp.minimum(per_step * j + k, last_k) * tf, f - tf), V7X_LANES)

    in_specs = [pl.BlockSpec(memory_space=pl.ANY), pl.BlockSpec((1, d), lambda i, j: (0, 0))]
    args = [h, norm_g.reshape(1, d)]
    for k in range(per_step):
        in_specs += [
            pl.BlockSpec((pl.Element(d), pl.Element(tf)), lambda i, j, s=start(k): (0, s(j))),
            pl.BlockSpec((pl.Element(d), pl.Element(tf)), lambda i, j, s=start(k): (0, s(j))),
            pl.BlockSpec((pl.Element(tf), pl.Element(d)), lambda i, j, s=start(k): (s(j), 0)),
        ]
        args += [wg, wu, wd]
    if final_norm:
        in_specs.append(pl.BlockSpec((1, d), lambda i, j: (0, 0)))
        args.append(final_g.reshape(1, d))
    cast_specs = _row_cast_specs(cast, (t // tm) * n_steps, lambda i, j: i * n_steps + j)
    outs = pl.pallas_call(
        functools.partial(_ffn_kernel, n_f=n_f, n_chunks=4, overlap=overlap, final_norm=final_norm,
                          n_cast=len(cast)),
        grid=(t // tm, n_steps),
        in_specs=in_specs + cast_specs,
        out_specs=[pl.BlockSpec(memory_space=pl.ANY)] + cast_specs,
        out_shape=[jax.ShapeDtypeStruct((t, d), F32)] + [jax.ShapeDtypeStruct(w.shape, BF16) for w in cast],
        scratch_shapes=[pltpu.VMEM((2, tm, d), F32), pltpu.VMEM((tm, d), BF16),
                        pltpu.SemaphoreType.DMA((2,)), pltpu.SemaphoreType.DMA((2,))],
        compiler_params=_params("arbitrary", "arbitrary"),
        name="ffn_final" if final_norm else "ffn",
    )(*args, *cast)
    return outs[0], outs[1:]


def _inproj_kernel(*refs, n_qkv, n_cast):
    h_ref, g_ref, w_ref = refs[:3]
    cast_in = refs[3:3 + n_cast]
    qkv_ref, u_ref = refs[3 + n_cast:5 + n_cast]
    cast_out = refs[5 + n_cast:5 + 2 * n_cast]
    n_ref = refs[5 + 2 * n_cast]
    cast_rows = functools.partial(_cast_rows, cast_in, cast_out)
    j = pl.program_id(1)

    @pl.when(j == 0)
    def _():
        n = _rmsnorm(h_ref[...], g_ref[...]).astype(BF16)
        n_ref[...] = n
        qkv_ref[...] = jnp.dot(n, w_ref[...], preferred_element_type=F32).astype(qkv_ref.dtype)
        cast_rows()

    @pl.when((j > 0) & (j < n_qkv))
    def _():
        qkv_ref[...] = jnp.dot(n_ref[...], w_ref[...], preferred_element_type=F32).astype(qkv_ref.dtype)
        cast_rows()

    @pl.when(j == n_qkv)
    def _():
        u_ref[...] = jnp.dot(n_ref[...], w_ref[...], preferred_element_type=F32)
        cast_rows()


def _inproj(h, norm_g, w_in, cast=(), *, tm=INPROJ_TOKENS):
    t, d = h.shape
    n_qkv = 3
    c = w_in.shape[1] // (n_qkv + 1)
    cast_specs = _row_cast_specs(cast, (t // tm) * (n_qkv + 1), lambda i, j: i * (n_qkv + 1) + j)
    outs = pl.pallas_call(
        functools.partial(_inproj_kernel, n_qkv=n_qkv, n_cast=len(cast)),
        grid=(t // tm, n_qkv + 1),
        in_specs=[
            pl.BlockSpec((tm, d), lambda i, j: (i, 0)),
            pl.BlockSpec((1, d), lambda i, j: (0, 0)),
            pl.BlockSpec((d, c), lambda i, j: (0, j)),
        ] + cast_specs,
        out_specs=[pl.BlockSpec((tm, c), lambda i, j: (i, jnp.minimum(j, n_qkv - 1))),
                   pl.BlockSpec((tm, c), lambda i, j: (i, 0))] + cast_specs,
        out_shape=[jax.ShapeDtypeStruct((t, n_qkv * c), BF16), jax.ShapeDtypeStruct((t, c), F32)]
        + [jax.ShapeDtypeStruct(w.shape, BF16) for w in cast],
        scratch_shapes=[pltpu.VMEM((tm, d), BF16)],
        compiler_params=_params("arbitrary", "arbitrary"),
        name="inproj",
    )(h, norm_g.reshape(1, d), w_in, *cast)
    return outs[0], outs[1], outs[2:]


def _ssm_discretise(log_dt, a_re, a_im, b_re, b_im):
    dt = jnp.exp(log_dt)[:, None]
    mag = jnp.exp(a_re * dt)
    lb_re = mag * jnp.cos(a_im * dt)
    lb_im = mag * jnp.sin(a_im * dt)
    den = a_re * a_re + a_im * a_im
    nr = lb_re - 1.0
    ni = lb_im
    f_re = (nr * a_re + ni * a_im) / den
    f_im = (ni * a_re - nr * a_im) / den
    bb_re = f_re[..., None] * b_re - f_im[..., None] * b_im
    bb_im = f_re[..., None] * b_im + f_im[..., None] * b_re
    return lb_re, lb_im, bb_re, bb_im


def _ssm_ops_kernel(*refs, lc, n_cast):
    pwr, pwi, bba, bbb, cca, crr, cii = refs[:7]
    krev_ref, w_ref, vre_ref, vim_ref = refs[7 + n_cast:11 + n_cast]
    _cast_rows(refs[7:7 + n_cast], refs[11 + n_cast:])
    _ssm_ops_body(pwr, pwi, bba, bbb, cca, crr, cii, krev_ref, w_ref, vre_ref, vim_ref, lc=lc)


def _ssm_ops_body(pwr, pwi, bba, bbb, cca, crr, cii, krev_ref, w_ref, vre_ref, vim_ref, *, lc):
    for gi in range(pwr.shape[0]):
        cmul = lambda k, a, b, gi=gi: pwr[gi, k:k + 1, :] * a + pwi[gi, k:k + 1, :] * b
        w = jnp.concatenate([cmul(lc - 1 - tau, bba[gi], bbb[gi]) for tau in range(lc)], axis=0)
        w_ref[gi] = w.astype(BF16)
        vre_ref[gi] = jnp.concatenate([cmul(t + 1, crr[gi], -cii[gi]) for t in range(lc)], axis=0).astype(BF16)
        vim_ref[gi] = jnp.concatenate([cmul(t + 1, -cii[gi], -crr[gi]) for t in range(lc)], axis=0).astype(BF16)
        krev_ref[gi] = lax.dot_general(cca[gi], w, (((1,), (1,)), ((), ())), precision=lax.Precision.HIGHEST,
                                       preferred_element_type=F32).astype(krev_ref.dtype)


def _toeplitz_kernel(*refs, group, lc, n_cast):
    krev_ref, m_ref = refs[0], refs[1 + n_cast]
    _cast_rows(refs[1:1 + n_cast], refs[2 + n_cast:])
    g, _, n = krev_ref.shape
    krev = krev_ref[...].reshape(g * group, n)
    diag = lax.broadcasted_iota(jnp.int32, (n, n), 0) - lax.broadcasted_iota(jnp.int32, (n, n), 1)
    for k in range(m_ref.shape[1]):
        t = pl.program_id(0) * m_ref.shape[1] + k
        shift = (diag == group * (lc - 1 - t)).astype(BF16)
        res = jnp.dot(krev, shift, preferred_element_type=F32)
        m_ref[:, k] = res.reshape(g, group, n).astype(BF16)


def _lag_rows(tau, n, lc):
    return pl.ds((tau % SSM_HOP) * (n // SSM_HOP) + tau // SSM_HOP, n // lc, stride=lc // SSM_HOP)


def _ssm_pack_kernel(u_ref, o_ref, hop_ref, *, lc):
    n = u_ref.shape[0]
    n_chunks = n // lc
    for dst, rows in _regroup(n, 1, SSM_HOP):
        hop_ref[dst, :] = u_ref[rows, :]
    for tau in range(lc):
        x = hop_ref[_lag_rows(tau, n, lc), :]
        o_ref[:, tau * n_chunks:(tau + 1) * n_chunks] = x.T.astype(o_ref.dtype)


def _ssm_unpack_kernel(y_ref, o_ref, hop_ref, *, lc):
    n = o_ref.shape[0]
    n_chunks = n // lc
    for t in range(lc):
        hop_ref[_lag_rows(t, n, lc), :] = y_ref[:, t * n_chunks:(t + 1) * n_chunks].T
    for src, rows in _regroup(n, 1, SSM_HOP):
        o_ref[rows, :] = hop_ref[src, :]


def _ssm_main_kernel(ut_ref, m_ref, w_ref, vre_ref, vim_ref, ar_ref, ai_ref, yt_ref, *, batch, lc, pack, n_steps):
    n_groups = m_ref.shape[0]
    group = ut_ref.shape[0] // n_groups
    seq = ut_ref.shape[1] // batch
    blocks = seq // (pack * lc)
    n_chunks = blocks * pack
    lanes = w_ref.shape[2]
    half = lanes // 2
    nt = (((1,), (1,)), ((), ()))
    chunk = lax.broadcasted_iota(jnp.int32, (n_chunks, lanes), 0)
    first = lax.broadcasted_iota(jnp.int32, (n_chunks, lanes), 1) < half
    first_v = lax.broadcasted_iota(jnp.int32, vre_ref.shape[1:], 1) < half
    lane0 = lambda b, blk, k: b * seq + blk * (pack * lc) + k * pack

    for gi in range(n_groups):
        rows = slice(gi * group, (gi + 1) * group)

        def load_xt(b, rows=rows):
            return jnp.concatenate(
                [jnp.concatenate([ut_ref[rows, pl.ds(lane0(b, blk, tau), pack)] for blk in range(blocks)], axis=1)
                 for tau in range(lc)], axis=0)

        for b0 in range(0, batch, 2):
            xts = [load_xt(b0), load_xt(b0 + 1)]
            sa, sb = [lax.dot_general(xt, w_ref[gi], (((0,), (0,)), ((), ())), preferred_element_type=F32)
                      for xt in xts]
            s_re = jnp.where(first, sa, pltpu.roll(sb, half, 1))
            s_im = jnp.where(first, pltpu.roll(sa, half, 1), sb)
            for j in range(n_steps):
                k = 1 << j
                shift = lambda x: jnp.where(chunk >= k, pltpu.roll(x, k, 0), 0.0)
                sh_re, sh_im = shift(s_re), shift(s_im)
                a_re, a_im = ar_ref[gi, j:j + 1, :], ai_ref[gi, j:j + 1, :]
                s_re, s_im = s_re + a_re * sh_re - a_im * sh_im, s_im + a_re * sh_im + a_im * sh_re
            enter = lambda x: jnp.where(chunk >= 1, pltpu.roll(x, 1, 0), 0.0).astype(BF16)
            in_re, in_im = enter(s_re), enter(s_im)
            for i, xt in enumerate(xts):
                own = first_v if i == 0 else ~first_v
                keep = lambda v: jnp.where(own, v, jnp.zeros_like(v))
                yt = (jnp.dot(m_ref[gi], xt, preferred_element_type=F32)
                      + lax.dot_general(keep(vre_ref[gi]), in_re, nt, preferred_element_type=F32)
                      + lax.dot_general(keep(vim_ref[gi]), in_im, nt, preferred_element_type=F32))
                for t in range(lc):
                    for blk in range(blocks):
                        yt_ref[rows, pl.ds(lane0(b0 + i, blk, t), pack)] = (
                            yt[t * group:(t + 1) * group, blk * pack:(blk + 1) * pack])


def _ssm_operators(seq, log_dt, a_re, a_im, b_re, b_im, c_re, c_im, cast_ops=(), cast_toeplitz=()):
    g, p = a_re.shape
    h = b_re.shape[-1]
    lc = SSM_CHUNK
    n = lc * h
    n_steps = max(1, (seq // lc - 1).bit_length())

    lb_re, lb_im, bb_re, bb_im = _ssm_discretise(log_dt, a_re, a_im, b_re, b_im)
    pw_r, pw_i = [jnp.ones_like(lb_re)], [jnp.zeros_like(lb_re)]
    for _ in range(lc):
        pw_r, pw_i = (pw_r + [pw_r[-1] * lb_re - pw_i[-1] * lb_im],
                      pw_i + [pw_r[-1] * lb_im + pw_i[-1] * lb_re])
    pw_r = jnp.stack(pw_r, axis=1)
    pw_i = jnp.stack(pw_i, axis=1)
    pair = lambda a, b: jnp.concatenate([a, b], axis=-1)
    bt_r, bt_i = bb_re.transpose(0, 2, 1), bb_im.transpose(0, 2, 1)
    tables = [pair(pw_r, pw_r), pair(pw_i, pw_i), pair(bt_r, bt_i), pair(-bt_i, bt_r),
              pair(c_re, -c_im), pair(c_re, c_re), pair(c_im, c_im)]
    sr, si = pw_r[:, lc], pw_i[:, lc]
    ar, ai = [], []
    for _ in range(n_steps):
        ar.append(pair(sr, sr))
        ai.append(pair(si, si))
        sr, si = sr * sr - si * si, 2.0 * sr * si
    ar = jnp.stack(ar, axis=1)
    ai = jnp.stack(ai, axis=1)

    gs = SSM_OPS_GROUPS
    assert g % gs == 0
    per_group = lambda *shape: pl.BlockSpec((gs,) + shape, lambda i: (i,) + (0,) * len(shape))
    cast_specs = _row_cast_specs(cast_ops, g // gs, lambda i: i)
    krev, w, vre, vim, *ops_casts = pl.pallas_call(
        functools.partial(_ssm_ops_kernel, lc=lc, n_cast=len(cast_ops)),
        grid=(g // gs,),
        in_specs=[per_group(lc + 1, 2 * p)] * 2 + [per_group(h, 2 * p)] * 5 + cast_specs,
        out_specs=[per_group(h, n)] + [per_group(n, 2 * p)] * 3 + cast_specs,
        out_shape=[jax.ShapeDtypeStruct((g, h, n), BF16)] + [jax.ShapeDtypeStruct((g, n, 2 * p), BF16)] * 3
        + [jax.ShapeDtypeStruct(x.shape, BF16) for x in cast_ops],
        compiler_params=_params("arbitrary"),
        name="ssm_ops",
    )(*tables, *cast_ops)

    n_rows = lc // SSM_TOEPLITZ_ROWS
    cast_specs = _row_cast_specs(cast_toeplitz, n_rows, lambda i: i)
    m, *toeplitz_casts = pl.pallas_call(
        functools.partial(_toeplitz_kernel, group=h, lc=lc, n_cast=len(cast_toeplitz)),
        grid=(n_rows,),
        in_specs=[pl.BlockSpec((g, h, n), lambda i: (0, 0, 0))] + cast_specs,
        out_specs=[pl.BlockSpec((g, SSM_TOEPLITZ_ROWS, h, n), lambda i: (0, i, 0, 0))] + cast_specs,
        out_shape=[jax.ShapeDtypeStruct((g, lc, h, n), BF16)]
        + [jax.ShapeDtypeStruct(x.shape, BF16) for x in cast_toeplitz],
        compiler_params=_params("arbitrary"),
        name="ssm_toeplitz",
    )(krev, *cast_toeplitz)
    return (m.reshape(g, n, n), w, vre, vim, ar, ai), ops_casts, toeplitz_casts


def _ssm_conv(u, batch, operators):
    m, w, vre, vim, ar, ai = operators
    t, d_ssm = u.shape
    g, n = m.shape[:2]
    p, n_steps = w.shape[2] // 2, ar.shape[1]
    lc = SSM_CHUNK
    h = n // lc
    seq = t // batch
    pack = V7X_LANES
    pb = pack * lc
    gm = SSM_MAIN_GROUPS
    assert seq % pb == 0 and d_ssm % V7X_LANES == 0 and batch % 2 == 0 and g % gm == 0
    per_group = lambda *shape, gs=1: pl.BlockSpec((gs,) + shape, lambda i: (i,) + (0,) * len(shape))

    tok_blk = pl.BlockSpec((pb, V7X_LANES), lambda i, j: (i, j))
    ch_blk = pl.BlockSpec((V7X_LANES, pb), lambda i, j: (j, i))
    relayout_grid = (t // pb, d_ssm // V7X_LANES)
    ut = pl.pallas_call(
        functools.partial(_ssm_pack_kernel, lc=lc),
        grid=relayout_grid,
        in_specs=[tok_blk],
        out_specs=ch_blk,
        out_shape=jax.ShapeDtypeStruct((d_ssm, t), BF16),
        scratch_shapes=[pltpu.VMEM((pb, V7X_LANES), F32)],
        compiler_params=_params("parallel", "parallel"),
        name="ssm_pack",
    )(u)
    yt = pl.pallas_call(
        functools.partial(_ssm_main_kernel, batch=batch, lc=lc, pack=pack, n_steps=n_steps),
        grid=(g // gm,),
        in_specs=[pl.BlockSpec((gm * h, t), lambda i: (i, 0)), per_group(n, n, gs=gm)]
        + [per_group(n, 2 * p, gs=gm)] * 3 + [per_group(n_steps, 2 * p, gs=gm)] * 2,
        out_specs=pl.BlockSpec((gm * h, t), lambda i: (i, 0)),
        out_shape=jax.ShapeDtypeStruct((d_ssm, t), F32),
        compiler_params=_params("parallel"),
        name="ssm_main",
    )(ut, m, w, vre, vim, ar, ai)
    return pl.pallas_call(
        functools.partial(_ssm_unpack_kernel, lc=lc),
        grid=relayout_grid,
        in_specs=[ch_blk],
        out_specs=tok_blk,
        out_shape=jax.ShapeDtypeStruct((t, d_ssm), F32),
        scratch_shapes=[pltpu.VMEM((pb, V7X_LANES), F32)],
        compiler_params=_params("parallel", "parallel"),
        name="ssm_unpack",
    )(yt)


def _regroup(n, dil, hop, piece=None):
    rows = n // (dil * hop)
    piece = piece or rows
    for res in range(dil * hop):
        a, r = divmod(res, dil)
        for k in range(0, rows, piece):
            yield (slice(res * rows + k, res * rows + k + piece),
                   pl.ds(r * (n // dil) + a + k * hop, piece, stride=hop))


def _attn_kernel(q_ref, k_ref, v_ref, o_ref, stage, kp4, vp4, kp16, vp16, qf, qp, ob, mb, lb, *, span, blk):
    s = pl.program_id(2)
    seq, lanes = k_ref.shape[1], k_ref.shape[2]
    dils = [d for _, d in DILATIONS]
    hop = dils[1]
    k_perm = {dils[1]: kp4, dils[2]: kp16}
    v_perm = {dils[1]: vp4, dils[2]: vp16}

    @pl.when(s == 0)
    def _():
        for src, perm in ((k_ref, k_perm), (v_ref, v_perm)):
            stage[0] = src[0].astype(F32)
            for dst, rows in _regroup(seq, dils[0], hop):
                x = stage[0, rows, :]
                stage[1, dst, :] = x
                perm[dils[1]][dst, :] = x.astype(BF16)
            for dst, rows in _regroup(seq, dils[1], hop):
                perm[dils[2]][dst, :] = stage[1, rows, :].astype(BF16)

    scale = math.log2(math.e) / math.sqrt(HEAD_DIM)
    q0 = q_ref[0].astype(F32) * scale
    qf[0] = q0
    qp[0] = q0.astype(BF16)
    for dst, rows in _regroup(span, dils[0], hop):
        x = qf[0, rows, :]
        qf[1, dst, :] = x
        qp[1, dst, :] = x.astype(BF16)
    for dst, rows in _regroup(span, dils[1], hop):
        qp[2, dst, :] = qf[1, rows, :].astype(BF16)

    head0 = lax.broadcasted_iota(jnp.int32, (blk, lanes), 1) < HEAD_DIM
    head0_kv = lax.broadcasted_iota(jnp.int32, (2 * blk, lanes), 1) < HEAD_DIM
    rel = (lax.broadcasted_iota(jnp.int32, (blk, 2 * blk), 0)
           - lax.broadcasted_iota(jnp.int32, (blk, 2 * blk), 1))

    def run_branches(first_span):
        band = None if first_span else (rel >= -blk) & (rel <= 0)
        for di, dil in enumerate(dils):
            per_res = span // dil
            n_blk = per_res // blk
            res_len = seq // dil

            def body(it, carry, di=di, dil=dil, per_res=per_res, n_blk=n_blk, res_len=res_len):
                res = it // n_blk
                jj = it % n_blk
                qb = s * per_res + jj * blk
                kb = jnp.maximum(qb - blk, 0)
                row0 = pl.multiple_of(res * per_res + jj * blk, blk)
                koff = pl.multiple_of(res * res_len + kb, blk)
                q = qp[di, pl.ds(row0, blk), :]
                if dil == 1:
                    kk = k_ref[0, pl.ds(koff, 2 * blk), :]
                    vv = v_ref[0, pl.ds(koff, 2 * blk), :]
                else:
                    kk = k_perm[dil][pl.ds(koff, 2 * blk), :]
                    vv = v_perm[dil][pl.ds(koff, 2 * blk), :]
                if first_span:
                    lo = kb - qb
                    valid = (rel >= lo) & (rel <= lo + blk)
                else:
                    valid = band
                outs, maxs = [], []
                for head_mask, kv_mask in ((head0, head0_kv), (~head0, ~head0_kv)):
                    qh = jnp.where(head_mask, q, jnp.zeros_like(q))
                    sc = lax.dot_general(qh, kk, (((1,), (1,)), ((), ())), preferred_element_type=F32)
                    sc = jnp.where(valid, sc, NEG_INF)
                    m = jnp.max(sc, axis=-1, keepdims=True)
                    p = jnp.exp2(sc - m)
                    maxs.append(m)
                    outs.append(jnp.dot(p.astype(BF16), jnp.where(kv_mask, vv, jnp.ones_like(vv)),
                                        preferred_element_type=F32))
                ob[di, pl.ds(row0, blk), :] = jnp.where(head0, outs[0], outs[1])
                mb[di, pl.ds(row0, blk), :] = jnp.where(head0, maxs[0], maxs[1])
                lb[di, pl.ds(row0, blk), :] = jnp.where(head0, outs[1], outs[0])
                return carry

            lax.fori_loop(0, dil * n_blk, body, 0, unroll=True)

    pl.when(s == 0)(functools.partial(run_branches, True))
    pl.when(s != 0)(functools.partial(run_branches, False))

    unswap = lambda x: pltpu.roll(x, HEAD_DIM, 1)
    piece = ATTN_MERGE_ROWS
    last = len(dils) - 1
    for di in range(last, 0, -1):
        for src, rows in _regroup(span, dils[di - 1], hop, piece):
            m_a = mb[di - 1, rows, :]
            m_b = mb[di, src, :]
            l_b = lb[di, src, :]
            if di == last:
                l_b = unswap(l_b)
            m_new = jnp.maximum(m_a, m_b)
            w_a = jnp.exp2(m_a - m_new)
            w_b = jnp.exp2(m_b - m_new)
            ob[di - 1, rows, :] = ob[di - 1, rows, :] * w_a + ob[di, src, :] * w_b
            lb[di - 1, rows, :] = unswap(lb[di - 1, rows, :]) * w_a + l_b * w_b
            mb[di - 1, rows, :] = m_new
    o_ref[0] = (ob[0] / lb[0]).astype(o_ref.dtype)


def _attention(qkv, batch):
    t, c = qkv.shape[0], qkv.shape[1] // 3
    seq = t // batch
    span, blk = ATTN_SPAN, ATTN_BLOCK
    dils = [d for _, d in DILATIONS]
    assert dils == [1, 4, 16] and all(w // d == blk and span % (blk * d) == 0 for w, d in DILATIONS)
    assert seq % span == 0 and V7X_LANES % HEAD_DIM == 0
    qkv3 = qkv.reshape(batch, seq, 3 * c)
    lanes = V7X_LANES
    n_hp = c // lanes
    full = lambda part: pl.BlockSpec((1, seq, lanes), lambda b, hp, s: (b, 0, part * n_hp + hp))
    tile = pl.BlockSpec((1, span, lanes), lambda b, hp, s: (b, s, hp))
    out = pl.pallas_call(
        functools.partial(_attn_kernel, span=span, blk=blk),
        grid=(batch, n_hp, seq // span),
        in_specs=[tile, full(1), full(2)],
        out_specs=tile,
        out_shape=jax.ShapeDtypeStruct((batch, seq, c), BF16),
        scratch_shapes=[pltpu.VMEM((2, seq, lanes), F32)]
        + [pltpu.VMEM((seq, lanes), BF16)] * 4
        + [pltpu.VMEM((2, span, lanes), F32), pltpu.VMEM((len(dils), span, lanes), BF16)]
        + [pltpu.VMEM((len(dils), span, lanes), F32)] * 3,
        compiler_params=_params("parallel", "parallel", "arbitrary"),
        name="attn",
    )(qkv3, qkv3, qkv3)
    return out.reshape(t, c)


def _post_kernel(yc_ref, u_ref, att_ref, h_ref, d_ref, wglu_ref, bglu_ref, gs_ref, ga_ref, wo_ref, o_ref):
    y = jax.nn.gelu(yc_ref[...] + d_ref[...] * u_ref[...])
    gate = jax.nn.sigmoid(jnp.dot(y.astype(BF16), wglu_ref[...], preferred_element_type=F32) + bglu_ref[...])
    ys = _rmsnorm(y * gate, gs_ref[...]).astype(BF16)
    ya = _rmsnorm(att_ref[...].astype(F32), ga_ref[...]).astype(BF16)
    d_ssm = ys.shape[1]
    o_ref[...] = (h_ref[...]
                  + jnp.dot(ys, wo_ref[:d_ssm, :], preferred_element_type=F32)
                  + jnp.dot(ya, wo_ref[d_ssm:, :], preferred_element_type=F32))


def _post(yc, u, att, h, d, w_glu, b_glu, g_ssm, g_att, w_out, *, tm=POST_TOKENS):
    t, dm = h.shape
    c = yc.shape[1]
    row = lambda i: (i, 0)
    const = lambda i: (0, 0)
    return pl.pallas_call(
        _post_kernel,
        grid=(t // tm,),
        in_specs=[pl.BlockSpec((tm, c), row), pl.BlockSpec((tm, c), row), pl.BlockSpec((tm, c), row),
                  pl.BlockSpec((tm, dm), row),
                  pl.BlockSpec((1, c), const), pl.BlockSpec((c, c), const), pl.BlockSpec((1, c), const),
                  pl.BlockSpec((1, c), const), pl.BlockSpec((1, c), const), pl.BlockSpec((2 * c, dm), const)],
        out_specs=pl.BlockSpec((tm, dm), row),
        out_shape=jax.ShapeDtypeStruct((t, dm), F32),
        compiler_params=_params("parallel"),
        name="post",
    )(yc, u, att, h, d.reshape(1, c), w_glu, b_glu.reshape(1, c), g_ssm.reshape(1, c), g_att.reshape(1, c), w_out)


def kernel(x, ffn1_norm, ffn1_w_gate, ffn1_w_up, ffn1_w_down, mix_norm, w_in, ssm_log_dt, ssm_a_re, ssm_a_im, ssm_b_re, ssm_b_im, ssm_c_re, ssm_c_im, ssm_d, ssm_w_glu, ssm_b_glu, ssm_out_norm, attn_out_norm, w_out, ffn2_norm, ffn2_w_gate, ffn2_w_up, ffn2_w_down, final_norm):
    batch, seq, d_model = x.shape
    depth = ffn1_norm.shape[0]
    h = x.reshape(batch * seq, d_model)
    for l in range(depth):
        last = l == depth - 1
        ssm_operators, (w_down_bf, w_gate_bf), (w_up_bf,) = _ssm_operators(
            seq, ssm_log_dt[l], ssm_a_re[l], ssm_a_im[l], ssm_b_re[l], ssm_b_im[l], ssm_c_re[l], ssm_c_im[l],
            cast_ops=(ffn1_w_down[l], ffn1_w_gate[l]), cast_toeplitz=(ffn1_w_up[l],))
        h, (w_in_bf,) = _ffn(h, ffn1_norm[l], w_gate_bf, w_up_bf, w_down_bf, cast=(w_in[l],))
        later = (ssm_w_glu[l], w_out[l], ffn2_w_gate[l], ffn2_w_up[l], ffn2_w_down[l])
        qkv, u, (w_glu_bf, w_out_bf, *ffn2_w_bf) = _inproj(h, mix_norm[l], w_in_bf, later)
        yc = _ssm_conv(u, batch, ssm_operators)
        att = _attention(qkv, batch)
        h = _post(yc, u, att, h, ssm_d[l], w_glu_bf, ssm_b_glu[l], ssm_out_norm[l], attn_out_norm[l], w_out_bf)
        h, _ = _ffn(h, ffn2_norm[l], *ffn2_w_bf, final_g=final_norm if last else None)
    return h.reshape(batch, seq, d_model)
```
